```python
import math
import jax, jax.numpy as jnp
from jax import lax
import numpy as np

D_MODEL = 1024
BATCH = 4
SEQ = 4096
DEPTH = 4

MIX_WIDTH = D_MODEL
POOL_WIDTH = D_MODEL // 2
POOL_WINDOWS = (2, 4, 8, 16)
POOL_GROUPS = len(POOL_WINDOWS)
POOL_GC = POOL_WIDTH // POOL_GROUPS
HEAD_DIM = 64
NSA_HEADS = (MIX_WIDTH - POOL_WIDTH) // HEAD_DIM
KV_GROUPS = 2
HEADS_PER_GROUP = NSA_HEADS // KV_GROUPS
N_BRANCH = 3
CMP_LEN = 32
CMP_STRIDE = 16
SEL_LEN = 64
SEL_TOPN = 16
WINDOW = 512
Q_BLOCK = 128
D_FF = 4 * D_MODEL
ROPE_THETA = 10000.0
EPS = 1e-6
NEG = -1e30
FORCED_SCORE = 1e4

Q_COLS = NSA_HEADS * HEAD_DIM
KV_COLS = KV_GROUPS * HEAD_DIM
GATE_COLS = NSA_HEADS * N_BRANCH
IN_COLS = POOL_WIDTH + Q_COLS + 2 * N_BRANCH * KV_COLS + GATE_COLS

kernel_name = "hybrid_pool_nsa_adaln_trunk"


def rms_norm(x, g):
    xf = x.astype(jnp.float32)
    y = xf * lax.rsqrt(jnp.mean(xf * xf, axis=-1, keepdims=True) + EPS)
    return (y * g.astype(jnp.float32)).astype(x.dtype)


def rope(x, pos):
    dk = x.shape[-1]
    half = dk // 2
    inv = ROPE_THETA ** (-jnp.arange(half, dtype=jnp.float32) * 2.0 / dk)
    ang = pos.astype(jnp.float32)[:, None] * inv[None, :]
    shape = (1, pos.shape[0]) + (1,) * (x.ndim - 3) + (half,)
    cos = jnp.cos(ang).reshape(shape)
    sin = jnp.sin(ang).reshape(shape)
    xf = x.astype(jnp.float32)
    x1, x2 = xf[..., :half], xf[..., half:]
    return jnp.concatenate([x1 * cos - x2 * sin, x2 * cos + x1 * sin], axis=-1).astype(x.dtype)


def pool_mixer(u, pool_w, pool_scale):
    B, S, _ = u.shape
    ug = u.reshape(B, S, POOL_GROUPS, POOL_GC).astype(jnp.float32)
    cs = jnp.concatenate([jnp.zeros((B, 1, POOL_GROUPS, POOL_GC), jnp.float32),
                          jnp.cumsum(ug, axis=1)], axis=1)
    t = jnp.arange(S)
    win = jnp.array(POOL_WINDOWS, dtype=jnp.int32)
    lower = jnp.maximum(t[:, None] + 1 - win[None, :], 0)
    count = jnp.minimum(t[:, None] + 1, win[None, :]).astype(jnp.float32)
    gidx = jnp.arange(POOL_GROUPS)[None, :]
    sums = cs[:, 1:] - cs[:, lower, gidx]
    pooled = (sums / count[None, :, :, None] - ug).astype(u.dtype)
    y = jnp.einsum('bsgc,gcd->bsgd', pooled, pool_w)
    return y.reshape(B, S, POOL_WIDTH) * pool_scale


def compress(t, pe, w1, w2, blk):
    B = t.shape[0]
    n_cmp = blk.shape[0]
    tb = t[:, blk] + pe[None, None, :, None, :]
    flat = tb.transpose(0, 1, 3, 2, 4).reshape(B, n_cmp, KV_GROUPS, CMP_LEN * HEAD_DIM)
    return jax.nn.silu(flat @ w1) @ w2


def nsa_mixer(q, k_all, v_all, gates, q_norm, k_norm, cmp_pe, cmp_w1, cmp_w2):
    B, S = q.shape[0], q.shape[1]
    G, Hg, dk = KV_GROUPS, HEADS_PER_GROUP, HEAD_DIM
    scale = 1.0 / math.sqrt(dk)
    pos = jnp.arange(S)

    q = rope(rms_norm(q, q_norm), pos).reshape(B, S, G, Hg, dk)
    k_s = rope(rms_norm(k_all[:, :, 1], k_norm[1]), pos)
    k_w = rope(rms_norm(k_all[:, :, 2], k_norm[2]), pos)
    v_s = v_all[:, :, 1]
    v_w = v_all[:, :, 2]

    n_cmp = (S - CMP_LEN) // CMP_STRIDE + 1
    blk = np.arange(n_cmp)[:, None] * CMP_STRIDE + np.arange(CMP_LEN)[None, :]
    cmp_end = jnp.arange(n_cmp) * CMP_STRIDE + (CMP_LEN - 1)
    k_cmp = compress(k_all[:, :, 0], cmp_pe[0], cmp_w1[0], cmp_w2[0], blk)
    v_cmp = compress(v_all[:, :, 0], cmp_pe[1], cmp_w1[1], cmp_w2[1], blk)
    k_cmp = rope(rms_norm(k_cmp, k_norm[0]), cmp_end)
    s_c = jnp.einsum('bsghd,bngd->bsghn', q, k_cmp).astype(jnp.float32) * scale
    mask_c = (cmp_end[None, :] <= pos[:, None])[None, :, None, None, :]
    p_c = jax.nn.softmax(jnp.where(mask_c, s_c, NEG), axis=-1)
    p_c = jnp.where(mask_c, p_c, 0.0)
    o_cmp = jnp.einsum('bsghn,bngd->bsghd', p_c.astype(v_cmp.dtype), v_cmp)

    n_sel = S // SEL_LEN
    cs0 = np.arange(n_cmp) * CMP_STRIDE
    ss0 = np.arange(n_sel) * SEL_LEN
    ov = np.minimum(cs0[:, None] + CMP_LEN, ss0[None, :] + SEL_LEN) - np.maximum(cs0[:, None], ss0[None, :])
    ov_mat = jnp.asarray(np.clip(ov, 0, None).astype(np.float32) / CMP_LEN)
    imp = jnp.einsum('bsghn,nj->bsgj', p_c, ov_mat)
    blk_id = jnp.arange(n_sel)[None, :]
    cur = (pos // SEL_LEN)[:, None]
    causal = blk_id <= cur
    forced = (blk_id == 0) | (blk_id == cur) | (blk_id == cur - 1)
    imp = jnp.where(causal[None, :, None, :],
                    jnp.where(forced[None, :, None, :], FORCED_SCORE, imp), -1.0)
    topn = min(SEL_TOPN, n_sel)
    vals, sel_idx = lax.top_k(imp, topn)
    sel_valid = vals >= 0.0

    k_sb = k_s.reshape(B, n_sel, SEL_LEN, G, dk).transpose(0, 3, 1, 2, 4)
    v_sb = v_s.reshape(B, n_sel, SEL_LEN, G, dk).transpose(0, 3, 1, 2, 4)
    pad = ((0, 0), (WINDOW, 0), (0, 0), (0, 0))
    k_wp = jnp.pad(k_w, pad)
    v_wp = jnp.pad(v_w, pad)
    bi = jnp.arange(B)[:, None, None, None]
    gi = jnp.arange(G)[None, None, :, None]
    C = Q_BLOCK

    def block_fn(qb):
        start = qb * C
        qc = lax.dynamic_slice_in_dim(q, start, C, axis=1)
        tc = start + jnp.arange(C)
        idx = lax.dynamic_slice_in_dim(sel_idx, start, C, axis=1)
        val = lax.dynamic_slice_in_dim(sel_valid, start, C, axis=1)
        kg = k_sb[bi, gi, idx]
        vg = v_sb[bi, gi, idx]
        s = jnp.einsum('bcghd,bcgnld->bcghnl', qc, kg).astype(jnp.float32) * scale
        kpos = idx[..., None] * SEL_LEN + jnp.arange(SEL_LEN)
        m = (kpos <= tc[None, :, None, None, None]) & val[..., None]
        s = jnp.where(m[:, :, :, None], s, NEG)
        sh = s.shape
        p = jax.nn.softmax(s.reshape(sh[:4] + (sh[4] * sh[5],)), axis=-1).reshape(sh)
        o_s = jnp.einsum('bcghnl,bcgnld->bcghd', p.astype(vg.dtype), vg)
        kw = lax.dynamic_slice_in_dim(k_wp, start, C + WINDOW, axis=1)
        vw = lax.dynamic_slice_in_dim(v_wp, start, C + WINDOW, axis=1)
        kpos_w = start - WINDOW + jnp.arange(C + WINDOW)
        diff = tc[:, None] - kpos_w[None, :]
        mw = ((diff >= 0) & (diff < WINDOW) & (kpos_w[None, :] >= 0))[None, :, None, None, :]
        sw = jnp.einsum('bcghd,bkgd->bcghk', qc, kw).astype(jnp.float32) * scale
        pw = jax.nn.softmax(jnp.where(mw, sw, NEG), axis=-1)
        o_w = jnp.einsum('bcghk,bkgd->bcghd', pw.astype(vw.dtype), vw)
        return o_s, o_w

    o_s, o_w = lax.map(block_fn, jnp.arange(S // C))
    o_s = o_s.transpose(1, 0, 2, 3, 4, 5).reshape(B, S, G, Hg, dk)
    o_w = o_w.transpose(1, 0, 2, 3, 4, 5).reshape(B, S, G, Hg, dk)

    g = jax.nn.sigmoid(gates.astype(jnp.float32)).reshape(B, S, G, Hg, N_BRANCH).astype(q.dtype)
    o = g[..., 0:1] * o_cmp + g[..., 1:2] * o_s + g[..., 2:3] * o_w
    return o.reshape(B, S, NSA_HEADS * dk)


def hybrid_layer(x, mod, norm1, norm2, w_in, pool_w, pool_scale, q_norm, k_norm,
                 cmp_pe, cmp_w1, cmp_w2, w_out, w_ff1, w_ff2):
    B, S, _ = x.shape
    shift1, scale1, gate1, shift2, scale2, gate2 = jnp.split(mod[:, None, :], 6, axis=-1)
    h = rms_norm(x, norm1) * (1.0 + scale1) + shift1
    proj = h @ w_in
    o0 = POOL_WIDTH
    o1 = o0 + Q_COLS
    o2 = o1 + 2 * N_BRANCH * KV_COLS
    u = proj[..., :o0]
    q = proj[..., o0:o1].reshape(B, S, NSA_HEADS, HEAD_DIM)
    kv = proj[..., o1:o2].reshape(B, S, 2, N_BRANCH, KV_GROUPS, HEAD_DIM)
    gates = proj[..., o2:]
    y_pool = pool_mixer(u, pool_w, pool_scale)
    y_nsa = nsa_mixer(q, kv[:, :, 0], kv[:, :, 1], gates, q_norm, k_norm, cmp_pe, cmp_w1, cmp_w2)
    mix = jnp.concatenate([y_pool, y_nsa], axis=-1) @ w_out
    x = x + gate1 * mix
    h2 = rms_norm(x, norm2) * (1.0 + scale2) + shift2
    ff = jnp.square(jax.nn.relu(h2 @ w_ff1)) @ w_ff2
    return x + gate2 * ff


def setup_inputs(seed: int = 0) -> dict:
    key = jax.random.key(seed)
    ks = jax.random.split(key, 20)
    f32 = jnp.float32
    L, D = DEPTH, D_MODEL
    nrm = lambda k, shape, s: jax.random.normal(k, shape, f32) * s
    return {
        "x": nrm(ks[0], (BATCH, SEQ, D), 1.0),
        "c": nrm(ks[1], (BATCH, D), 1.0),
        "w_mod": nrm(ks[2], (L, D, 6 * D), 0.5 * D ** -0.5),
        "b_mod": nrm(ks[3], (L, 6 * D), 0.01),
        "norm1": 1.0 + nrm(ks[4], (L, D), 0.05),
        "norm2": 1.0 + nrm(ks[5], (L, D), 0.05),
        "w_in": nrm(ks[6], (L, D, IN_COLS), D ** -0.5),
        "pool_w": nrm(ks[7], (L, POOL_GROUPS, POOL_GC, POOL_GC), POOL_GC ** -0.5),
        "pool_scale": 0.5 + nrm(ks[8], (L, POOL_WIDTH), 0.1),
        "q_norm": 1.0 + nrm(ks[9], (L, HEAD_DIM), 0.05),
        "k_norm": 1.0 + nrm(ks[10], (L, N_BRANCH, HEAD_DIM), 0.05),
        "cmp_pe": nrm(ks[11], (L, 2, CMP_LEN, HEAD_DIM), 0.5),
        "cmp_w1": nrm(ks[12], (L, 2, CMP_LEN * HEAD_DIM, HEAD_DIM), (CMP_LEN * HEAD_DIM) ** -0.5),
        "cmp_w2": nrm(ks[13], (L, 2, HEAD_DIM, HEAD_DIM), HEAD_DIM ** -0.5),
        "w_out": nrm(ks[14], (L, MIX_WIDTH, D), MIX_WIDTH ** -0.5),
        "w_ff1": nrm(ks[15], (L, D, D_FF), D ** -0.5),
        "w_ff2": nrm(ks[16], (L, D_FF, D), D_FF ** -0.5),
    }


def reference(x, c, w_mod, b_mod, norm1, norm2, w_in, pool_w, pool_scale, q_norm, k_norm,
              cmp_pe, cmp_w1, cmp_w2, w_out, w_ff1, w_ff2):
    c_act = jax.nn.silu(c)
    for l in range(DEPTH):
        mod = c_act @ w_mod[l] + b_mod[l]
        x = hybrid_layer(x, mod, norm1[l], norm2[l], w_in[l], pool_w[l], pool_scale[l],
                         q_norm[l], k_norm[l], cmp_pe[l], cmp_w1[l], cmp_w2[l],
                         w_out[l], w_ff1[l], w_ff2[l])
    return x
```

```python
import functools
import math

import numpy as np
import jax
import jax.numpy as jnp
from jax import lax
from jax.experimental import pallas as pl
from jax.experimental.pallas import tpu as pltpu

F32 = jnp.float32
BF16 = jnp.bfloat16

LANES = 128
HEAD_DIM = 64
N_HEADS = 8
KV_GROUPS = 2
HEADS_PER_GROUP = N_HEADS // KV_GROUPS
N_BRANCH = 3
POOL_WINDOWS = (2, 4, 8, 16)
POOL_HALO = 16
CMP_LEN = 32
CMP_STRIDE = 16
SEL_LEN = 64
SEL_SLOTS = 64
SEL_TOPN = 16
WINDOW = 512
ROPE_THETA = 10000.0
EPS = 1e-6
NEG = -1e30
FORCED_SCORE = 1e4
SCALE = 1.0 / math.sqrt(HEAD_DIM)
VMEM_LIMIT = 56 * 1024 * 1024

NT_DIMS = (((1,), (1,)), ((), ()))


def _dot(a, b):
    return jnp.dot(a, b, preferred_element_type=F32)


def _dot_nt(a, b):
    return lax.dot_general(a, b, NT_DIMS, preferred_element_type=F32)


def _split_bf16(v):
    hi = v.astype(BF16)
    lo = (v - hi.astype(F32)).astype(BF16)
    return hi, lo


def _pack_heads(o, tq):
    low = lax.broadcasted_iota(jnp.int32, (tq, LANES), 1) < HEAD_DIM
    pairs = []
    for hp in range(HEADS_PER_GROUP // 2):
        a = o[(2 * hp) * tq:(2 * hp + 1) * tq]
        b = o[(2 * hp + 1) * tq:(2 * hp + 2) * tq]
        pairs.append(jnp.where(low, a, pltpu.roll(b, HEAD_DIM, axis=1)))
    return jnp.concatenate(pairs, axis=1)


def _mod_kernel(c_ref, w_ref, b_ref, o_ref, *, batch):
    c = c_ref[...]
    rows = c.shape[0]
    ca = c * jax.nn.sigmoid(c)
    hi = ca.astype(BF16).astype(F32)
    row = lax.broadcasted_iota(jnp.int32, ca.shape, 0)
    a = jnp.where(row < batch, hi, ca - hi).astype(BF16)
    w = w_ref[0]
    w_hi, w_lo = _split_bf16(w)
    r1 = _dot(a, w_hi)
    r2 = _dot(a, w_lo)
    o_ref[0] = r1 + pltpu.roll(r1, rows - batch, axis=0) + r2 + b_ref[0]


def _modulation(c, w_mod, b_mod):
    depth, d, n = w_mod.shape
    batch = c.shape[0]
    rows = -(-2 * batch // 8) * 8
    c2 = jnp.zeros((rows, d), F32).at[:batch].set(c).at[batch:2 * batch].set(c)
    tn = 1536
    out = pl.pallas_call(
        functools.partial(_mod_kernel, batch=batch),
        grid=(depth, n // tn),
        in_specs=[pl.BlockSpec((rows, d), lambda l, j: (0, 0)),
                  pl.BlockSpec((1, d, tn), lambda l, j: (l, 0, j)),
                  pl.BlockSpec((1, 1, tn), lambda l, j: (l, 0, j))],
        out_specs=pl.BlockSpec((1, rows, tn), lambda l, j: (l, 0, j)),
        out_shape=jax.ShapeDtypeStruct((depth, rows, n), F32),
        compiler_params=pltpu.CompilerParams(vmem_limit_bytes=VMEM_LIMIT),
    )(c2, w_mod, b_mod.reshape(depth, 1, n))
    return out[:, :batch].reshape(depth, batch, 6, d)


C_U = 0
C_Q = 512
C_QR = 1024
C_KV = 1536
C_KR = 2304
C_G = 2560
C_END = 2688


def _inproj_kernel(x_ref, mod_ref, n1_ref, w_ref, cos_ref, sin_ref, gq_ref, gk_ref, bd_ref,
                   u_ref, q_ref, ks_ref, kw_ref, vs_ref, vw_ref, kc_ref, vc_ref, g_ref, *, tm):
    i = pl.program_id(1)
    x = x_ref[0]
    shift = mod_ref[0, 0, 0:1, :]
    scale = mod_ref[0, 0, 1:2, :]
    ms = jnp.mean(x * x, axis=-1, keepdims=True)
    h = (x * lax.rsqrt(ms + EPS) * n1_ref[...]) * (1.0 + scale) + shift
    hb = h.astype(BF16)

    def proj(a, b):
        return _dot(hb, w_ref[:, a:b])

    def seg_mean(v, bdm):
        hi, lo = _split_bf16(v * v)
        return _dot(hi, bdm) + _dot(lo, bdm)

    u_ref[0] = proj(C_U, C_Q)

    cos = cos_ref[...]
    sin = sin_ref[...]
    cos4 = jnp.concatenate([cos] * 4, axis=1)
    sin4 = jnp.concatenate([sin] * 4, axis=1)
    q = proj(C_Q, C_QR)
    qr = proj(C_QR, C_KV)
    rq = lax.rsqrt(seg_mean(q, bd_ref[...]) + EPS)
    qo = (q * gq_ref[0:1, :] * cos4 + qr * gq_ref[1:2, :] * sin4) * (rq * SCALE)

    lane = lax.broadcasted_iota(jnp.int32, (tm, LANES), 1)
    low = lane < HEAD_DIM
    for hh in range(N_HEADS):
        sl = qo[:, LANES * (hh // 2):LANES * (hh // 2 + 1)]
        if hh % 2:
            sl = pltpu.roll(sl, HEAD_DIM, axis=1)
        q_ref[0, hh] = jnp.where(low, sl, 0.0).astype(BF16)

    kv = proj(C_KV, C_KR)
    kr = proj(C_KR, C_G)
    bd2 = bd_ref[0:LANES, 0:LANES]
    k_c, k_s, k_w = kv[:, 0:128], kv[:, 128:256], kv[:, 256:384]
    v_c, v_s, v_w = kv[:, 384:512], kv[:, 512:640], kv[:, 640:768]
    rs = lax.rsqrt(seg_mean(k_s, bd2) + EPS)
    kso = (k_s * gk_ref[0:1, :] * cos + kr[:, 0:128] * gk_ref[1:2, :] * sin) * rs
    rw = lax.rsqrt(seg_mean(k_w, bd2) + EPS)
    kwo = (k_w * gk_ref[2:3, :] * cos + kr[:, 128:256] * gk_ref[3:4, :] * sin) * rw

    pos = i * tm + lax.broadcasted_iota(jnp.int32, (tm, LANES), 0)
    blk = pos // SEL_LEN
    blk_flag = jnp.where(lane - HEAD_DIM == blk, -NEG, 0.0)
    ones_col = jnp.where(lane == HEAD_DIM, 1.0, 0.0)
    for g in range(KV_GROUPS):
        def grp(a, g=g):
            return pltpu.roll(a, HEAD_DIM, axis=1) if g else a
        ks_ref[0, g] = jnp.where(low, grp(kso), blk_flag).astype(BF16)
        kw_ref[0, g] = jnp.where(low, grp(kwo), 0.0).astype(BF16)
        vs_ref[0, g] = jnp.where(low, grp(v_s), ones_col).astype(BF16)
        vw_ref[0, g] = jnp.where(low, grp(v_w), ones_col).astype(BF16)
        kc_ref[0, g] = grp(k_c)[:, 0:HEAD_DIM].astype(BF16)
        vc_ref[0, g] = grp(v_c)[:, 0:HEAD_DIM].astype(BF16)
    g_ref[0] = jax.nn.sigmoid(proj(C_G, C_END))


def _inproj(x, mod_l, layer, n1, w, cos, sin, gq, gk, bd, tm):
    B, S, D = x.shape
    kern = functools.partial(_inproj_kernel, tm=tm)
    grp_spec = lambda width: pl.BlockSpec((1, KV_GROUPS, tm, width), lambda b, i: (b, 0, i, 0))
    grp_shape = lambda width: jax.ShapeDtypeStruct((B, KV_GROUPS, S, width), BF16)
    const = lambda shape: pl.BlockSpec(shape, lambda b, i: (0,) * len(shape))
    return pl.pallas_call(
        kern,
        grid=(B, S // tm),
        in_specs=[pl.BlockSpec((1, tm, D), lambda b, i: (b, i, 0)),
                  pl.BlockSpec((1, 1, 6, D), lambda b, i: (layer, b, 0, 0)),
                  const((1, D)),
                  const((D, C_END)),
                  pl.BlockSpec((tm, LANES), lambda b, i: (i, 0)),
                  pl.BlockSpec((tm, LANES), lambda b, i: (i, 0)),
                  const((2, 512)),
                  const((4, LANES)),
                  const((512, 512))],
        out_specs=[pl.BlockSpec((1, tm, 512), lambda b, i: (b, i, 0)),
                   pl.BlockSpec((1, N_HEADS, tm, LANES), lambda b, i: (b, 0, i, 0)),
                   grp_spec(LANES), grp_spec(LANES), grp_spec(LANES), grp_spec(LANES),
                   grp_spec(HEAD_DIM), grp_spec(HEAD_DIM),
                   pl.BlockSpec((1, tm, LANES), lambda b, i: (b, i, 0))],
        out_shape=[jax.ShapeDtypeStruct((B, S, 512), F32),
                   jax.ShapeDtypeStruct((B, N_HEADS, S, LANES), BF16),
                   grp_shape(LANES), grp_shape(LANES), grp_shape(LANES), grp_shape(LANES),
                   grp_shape(HEAD_DIM), grp_shape(HEAD_DIM),
                   jax.ShapeDtypeStruct((B, S, LANES), F32)],
        compiler_params=pltpu.CompilerParams(vmem_limit_bytes=VMEM_LIMIT),
    )(x, mod_l, n1, w, cos, sin, gq, gk, bd)


def _compress_kernel(xk_ref, xv_ref, w1_ref, pe_ref, w2k_ref, w2v_ref, kn_ref, cos_ref, sin_ref,
                     ko_ref, vo_ref):
    nc = xk_ref.shape[1]
    half = (CMP_LEN // 2) * HEAD_DIM

    def hidden(x, w1, pe):
        w1_hi, w1_lo = _split_bf16(w1)
        pe_hi, pe_lo = _split_bf16(pe)
        bias = _dot(pe_hi, w1_hi) + _dot(pe_lo, w1_hi) + _dot(pe_hi, w1_lo)
        a = _dot(x, w1_hi[0:half])
        b = _dot(x, w1_hi[half:2 * half])
        pre = a + pltpu.roll(b, nc - 1, axis=0) + bias[0:1, :]
        return (pre * jax.nn.sigmoid(pre)).astype(BF16)

    hk = hidden(xk_ref[0], w1_ref[0], pe_ref[0])
    kk = _dot(hk, w2k_ref[...])
    k = kk[:, 0:LANES]
    kr = kk[:, LANES:2 * LANES]
    r = lax.rsqrt(jnp.sum(k * k, axis=-1, keepdims=True) * (1.0 / HEAD_DIM) + EPS)
    ko = (k * kn_ref[0:1, :] * cos_ref[...] + kr * kn_ref[1:2, :] * sin_ref[...]) * r
    ko_ref[0] = ko.astype(BF16)
    hv = hidden(xv_ref[0], w1_ref[1], pe_ref[1])
    vo_ref[0] = _dot(hv, w2v_ref[...]).astype(BF16)


def _compress(xk, xv, w1, pe, w2k, w2v, kn, cos_c, sin_c):
    n, nc, width = xk.shape
    const = lambda shape: pl.BlockSpec(shape, lambda j: (0,) * len(shape))
    out = jax.ShapeDtypeStruct((n, nc, LANES), BF16)
    return pl.pallas_call(
        _compress_kernel,
        grid=(n,),
        in_specs=[pl.BlockSpec((1, nc, width), lambda j: (j, 0, 0)),
                  pl.BlockSpec((1, nc, width), lambda j: (j, 0, 0)),
                  const(w1.shape), const(pe.shape), const(w2k.shape), const(w2v.shape),
                  const(kn.shape), const(cos_c.shape), const(sin_c.shape)],
        out_specs=[pl.BlockSpec((1, nc, LANES), lambda j: (j, 0, 0)),
                   pl.BlockSpec((1, nc, LANES), lambda j: (j, 0, 0))],
        out_shape=[out, out],
        compiler_params=pltpu.CompilerParams(vmem_limit_bytes=VMEM_LIMIT),
    )(xk, xv, w1, pe, w2k, w2v, kn, cos_c, sin_c)


def _cmp_attn_kernel(q_ref, kc_ref, vc_ref, ovt_ref, o_ref, qs_ref, *, tq, topn):
    i = pl.program_id(2)
    rows = HEADS_PER_GROUP * tq
    q = q_ref[0].reshape(rows, LANES)
    kc = kc_ref[0]
    nc = kc.shape[0]
    s = _dot_nt(q, kc)
    t = i * tq + (lax.broadcasted_iota(jnp.int32, (rows, nc), 0) & (tq - 1))
    n = lax.broadcasted_iota(jnp.int32, (rows, nc), 1)
    mask = n * CMP_STRIDE + (CMP_LEN - 1) <= t
    s = jnp.where(mask, s, NEG)
    m = jnp.max(s, axis=-1, keepdims=True)
    p = jnp.where(mask, jnp.exp(s - m), 0.0)
    l = jnp.sum(p, axis=-1, keepdims=True)
    p = p * (1.0 / jnp.where(l > 0.0, l, 1.0))
    o = _dot(p.astype(BF16), vc_ref[0])
    o_ref[0] = _pack_heads(o, tq)

    ps = p[0:tq]
    for hh in range(1, HEADS_PER_GROUP):
        ps = ps + p[hh * tq:(hh + 1) * tq]
    hi, lo = _split_bf16(ps)
    ovt = ovt_ref[...]
    imp_t = _dot_nt(ovt, hi) + _dot_nt(ovt, lo)
    imp = imp_t[SEL_SLOTS:2 * SEL_SLOTS]
    j = lax.broadcasted_iota(jnp.int32, (SEL_SLOTS, tq), 0)
    cur = (i * tq + lax.broadcasted_iota(jnp.int32, (SEL_SLOTS, tq), 1)) // SEL_LEN
    forced = (j == 0) | (j == cur) | (j == cur - 1)
    imp = jnp.where(j <= cur, jnp.where(forced, FORCED_SCORE, imp), -1.0)
    sub = 8
    slabs = [imp[a:a + sub] for a in range(0, SEL_SLOTS, sub)]
    ranks = [jnp.zeros((sub, tq), F32) for _ in slabs]
    sub_row = lax.broadcasted_iota(jnp.int32, (sub, tq), 0)
    for jj in range(SEL_SLOTS):
        rowj = jnp.broadcast_to(imp[jj:jj + 1, :], (sub, tq))
        for a, xs in enumerate(slabs):
            if a * sub > jj:
                inc = jnp.where(rowj >= xs, 1.0, 0.0)
            elif a * sub + sub - 1 <= jj:
                inc = jnp.where(rowj > xs, 1.0, 0.0)
            else:
                tie = jnp.where(sub_row > jj - a * sub, 1.0, 0.0)
                inc = jnp.where(rowj > xs, 1.0, jnp.where(rowj == xs, tie, 0.0))
            ranks[a] = ranks[a] + inc
    rank = jnp.concatenate(ranks, axis=0)
    keep = (rank < float(topn)) & (imp >= 0.0)
    flag = jnp.where(keep, 0.0, -1.0)
    flag_t = jnp.concatenate([jnp.zeros((SEL_SLOTS, tq), F32), flag], axis=0).T
    fb = flag_t.astype(BF16)
    for hh in range(HEADS_PER_GROUP):
        qs_ref[0, hh] = q_ref[0, hh] + fb


def _cmp_attn(q, kcmp, vcmp, ovt, tq, topn):
    B, H, S, _ = q.shape
    nc = kcmp.shape[1]
    kern = functools.partial(_cmp_attn_kernel, tq=tq, topn=topn)
    return pl.pallas_call(
        kern,
        grid=(B, KV_GROUPS, S // tq),
        in_specs=[pl.BlockSpec((1, HEADS_PER_GROUP, tq, LANES), lambda b, g, i: (b, g, i, 0)),
                  pl.BlockSpec((1, nc, LANES), lambda b, g, i: (b * KV_GROUPS + g, 0, 0)),
                  pl.BlockSpec((1, nc, LANES), lambda b, g, i: (b * KV_GROUPS + g, 0, 0)),
                  pl.BlockSpec((LANES, nc), lambda b, g, i: (0, 0))],
        out_specs=[pl.BlockSpec((1, tq, 256), lambda b, g, i: (b, i, g)),
                   pl.BlockSpec((1, HEADS_PER_GROUP, tq, LANES), lambda b, g, i: (b, g, i, 0))],
        out_shape=[jax.ShapeDtypeStruct((B, S, 512), F32),
                   jax.ShapeDtypeStruct((B, H, S, LANES), BF16)],
        compiler_params=pltpu.CompilerParams(vmem_limit_bytes=VMEM_LIMIT),
    )(q, kcmp, vcmp, ovt)


def _sel_attn_kernel(q_ref, k_ref, v_ref, o_ref, m_sc, acc_sc, *, tq, tk):
    i = pl.program_id(2)
    rows = HEADS_PER_GROUP * tq
    q = q_ref[0].reshape(rows, LANES)
    m_sc[...] = jnp.full(m_sc.shape, NEG, F32)
    acc_sc[...] = jnp.zeros(acc_sc.shape, F32)

    def step(kt, diagonal):
        start = pl.multiple_of(kt * tk, tk)
        k = k_ref[0, 0, pl.ds(start, tk), :]
        v = v_ref[0, 0, pl.ds(start, tk), :]
        s = _dot_nt(q, k)
        if diagonal:
            t = i * tq + (lax.broadcasted_iota(jnp.int32, (rows, tk), 0) & (tq - 1))
            key = kt * tk + lax.broadcasted_iota(jnp.int32, (rows, tk), 1)
            s = jnp.where(key <= t, s, NEG)
        m_prev = m_sc[...]
        m_new = jnp.maximum(m_prev, jnp.max(s, axis=-1, keepdims=True))
        alpha = jnp.exp(m_prev - m_new)
        p = jnp.exp(s - jnp.concatenate([m_new] * (tk // LANES), axis=1))
        acc_sc[...] = acc_sc[...] * alpha + _dot(p.astype(BF16), v)
        m_sc[...] = m_new

    n_full = (i * tq) // tk

    def body(kt, carry):
        step(kt, False)
        return carry

    lax.fori_loop(0, n_full, body, 0)
    step(n_full, True)
    acc = acc_sc[...]
    o = acc * (1.0 / acc[:, HEAD_DIM:HEAD_DIM + 1])
    o_ref[0] = _pack_heads(o, tq)


def _sel_attn(qs, ks, vs, tq, tk):
    B, H, S, _ = qs.shape
    kern = functools.partial(_sel_attn_kernel, tq=tq, tk=tk)
    rows = HEADS_PER_GROUP * tq
    return pl.pallas_call(
        kern,
        grid=(B, KV_GROUPS, S // tq),
        in_specs=[pl.BlockSpec((1, HEADS_PER_GROUP, tq, LANES), lambda b, g, i: (b, g, i, 0)),
                  pl.BlockSpec((1, 1, S, LANES), lambda b, g, i: (b, g, 0, 0)),
                  pl.BlockSpec((1, 1, S, LANES), lambda b, g, i: (b, g, 0, 0))],
        out_specs=pl.BlockSpec((1, tq, 256), lambda b, g, i: (b, i, g)),
        out_shape=jax.ShapeDtypeStruct((B, S, 512), F32),
        scratch_shapes=[pltpu.VMEM((rows, LANES), F32), pltpu.VMEM((rows, LANES), F32)],
        compiler_params=pltpu.CompilerParams(vmem_limit_bytes=VMEM_LIMIT),
    )(qs, ks, vs)


def _win_attn_kernel(q_ref, k_ref, v_ref, o_ref, *, tq, nwin):
    i = pl.program_id(2)
    rows = HEADS_PER_GROUP * tq
    q = q_ref[0].reshape(rows, LANES)
    r = lax.broadcasted_iota(jnp.int32, (rows, tq), 0) & (tq - 1)
    kk = lax.broadcasted_iota(jnp.int32, (rows, tq), 1)
    far = 1 << 20
    scores, starts = [], []
    for c in range(nwin + 1):
        ci = i - nwin + c
        off = jnp.where(ci >= 0, 0, far)
        start = pl.multiple_of(jnp.maximum(ci, 0) * tq, tq)
        s = _dot_nt(q, k_ref[0, 0, pl.ds(start, tq), :])
        if c == 0:
            mask = kk > r + off
        elif c == nwin:
            mask = kk <= r
        else:
            mask = kk >= off
        scores.append(jnp.where(mask, s, NEG))
        starts.append(start)
    m = scores[0].max(axis=-1, keepdims=True)
    for s in scores[1:]:
        m = jnp.maximum(m, s.max(axis=-1, keepdims=True))
    acc = jnp.zeros((rows, LANES), F32)
    for s, start in zip(scores, starts):
        p = jnp.exp(s - m)
        acc = acc + _dot(p.astype(BF16), v_ref[0, 0, pl.ds(start, tq), :])
    o = acc * (1.0 / acc[:, HEAD_DIM:HEAD_DIM + 1])
    o_ref[0] = _pack_heads(o, tq)


def _win_attn(qs, kw, vw, tq):
    B, H, S, _ = qs.shape
    kern = functools.partial(_win_attn_kernel, tq=tq, nwin=WINDOW // tq)
    return pl.pallas_call(
        kern,
        grid=(B, KV_GROUPS, S // tq),
        in_specs=[pl.BlockSpec((1, HEADS_PER_GROUP, tq, LANES), lambda b, g, i: (b, g, i, 0)),
                  pl.BlockSpec((1, 1, S, LANES), lambda b, g, i: (b, g, 0, 0)),
                  pl.BlockSpec((1, 1, S, LANES), lambda b, g, i: (b, g, 0, 0))],
        out_specs=pl.BlockSpec((1, tq, 256), lambda b, g, i: (b, i, g)),
        out_shape=jax.ShapeDtypeStruct((B, S, 512), F32),
        compiler_params=pltpu.CompilerParams(vmem_limit_bytes=VMEM_LIMIT),
    )(qs, kw, vw)


def _pool_kernel(u_ref, up_ref, pw_ref, psc_ref, y_ref, ext_sc, *, tm):
    i = pl.program_id(1)
    ext_sc[0:POOL_HALO, :] = jnp.where(i > 0, up_ref[0], 0.0)
    ext_sc[POOL_HALO:POOL_HALO + tm, :] = u_ref[0]
    pos = i * tm + lax.broadcasted_iota(jnp.int32, (tm, LANES), 0)
    for g, w in enumerate(POOL_WINDOWS):
        c0, c1 = g * LANES, (g + 1) * LANES
        cur = ext_sc[POOL_HALO:POOL_HALO + tm, c0:c1]
        acc = cur
        for k in range(1, w):
            acc = acc + ext_sc[POOL_HALO - k:POOL_HALO - k + tm, c0:c1]
        cnt = jnp.minimum(pos + 1, w).astype(F32)
        pooled = acc / cnt - cur
        y = _dot(pooled.astype(BF16), pw_ref[g]) * psc_ref[:, c0:c1]
        y_ref[0, :, c0:c1] = y.astype(BF16)


def _pool(u, pw, psc, tm):
    B, S, width = u.shape
    kern = functools.partial(_pool_kernel, tm=tm)
    per = tm // POOL_HALO
    return pl.pallas_call(
        kern,
        grid=(B, S // tm),
        in_specs=[pl.BlockSpec((1, tm, width), lambda b, i: (b, i, 0)),
                  pl.BlockSpec((1, POOL_HALO, width), lambda b, i: (b, jnp.maximum(i * per - 1, 0), 0)),
                  pl.BlockSpec(pw.shape, lambda b, i: (0, 0, 0)),
                  pl.BlockSpec((1, width), lambda b, i: (0, 0))],
        out_specs=pl.BlockSpec((1, tm, width), lambda b, i: (b, i, 0)),
        out_shape=jax.ShapeDtypeStruct((B, S, width), BF16),
        scratch_shapes=[pltpu.VMEM((tm + POOL_HALO, width), F32)],
        compiler_params=pltpu.CompilerParams(vmem_limit_bytes=VMEM_LIMIT),
    )(u, u, pw, psc)


def _out_ffn_kernel(x_ref, mod_ref, n2_ref, yp_ref, oc_ref, os_ref, ow_ref, g_ref, ge_ref,
                    wo_ref, w1_ref, w2_ref, o_ref, *, ff_chunk):
    x = x_ref[0]
    gate1 = mod_ref[0, 0, 2:3, :]
    shift2 = mod_ref[0, 0, 3:4, :]
    scale2 = mod_ref[0, 0, 4:5, :]
    gate2 = mod_ref[0, 0, 5:6, :]
    g_hi, g_lo = _split_bf16(g_ref[0])
    ge = ge_ref[...]
    gx = _dot(g_hi, ge) + _dot(g_lo, ge)
    y_nsa = gx[:, 0:512] * oc_ref[0] + gx[:, 512:1024] * os_ref[0] + gx[:, 1024:1536] * ow_ref[0]
    mix = _dot(yp_ref[0], wo_ref[0:512, :]) + _dot(y_nsa.astype(BF16), wo_ref[512:1024, :])
    x1 = x + gate1 * mix
    ms = jnp.mean(x1 * x1, axis=-1, keepdims=True)
    h2 = ((x1 * lax.rsqrt(ms + EPS) * n2_ref[...]) * (1.0 + scale2) + shift2).astype(BF16)
    d_ff = w1_ref.shape[1]
    ff = jnp.zeros(x.shape, F32)
    for c in range(d_ff // ff_chunk):
        a = jnp.maximum(_dot(h2, w1_ref[:, c * ff_chunk:(c + 1) * ff_chunk]), 0.0)
        ff = ff + _dot((a * a).astype(BF16), w2_ref[c * ff_chunk:(c + 1) * ff_chunk, :])
    o_ref[0] = x1 + gate2 * ff


def _out_ffn(x, mod_l, layer, n2, yp, oc, os_, ow, gates, ge, wo, w1, w2, tm):
    B, S, D = x.shape
    kern = functools.partial(_out_ffn_kernel, ff_chunk=1024)
    tok = lambda width: pl.BlockSpec((1, tm, width), lambda b, i: (b, i, 0))
    const = lambda shape: pl.BlockSpec(shape, lambda b, i: (0,) * len(shape),
                                       pipeline_mode=pl.Buffered(1))
    return pl.pallas_call(
        kern,
        grid=(B, S // tm),
        in_specs=[tok(D),
                  pl.BlockSpec((1, 1, 6, D), lambda b, i: (layer, b, 0, 0)),
                  const((1, D)),
                  tok(512), tok(512), tok(512), tok(512), tok(LANES),
                  const(ge.shape), const(wo.shape), const(w1.shape), const(w2.shape)],
        out_specs=tok(D),
        out_shape=jax.ShapeDtypeStruct((B, S, D), F32),
        compiler_params=pltpu.CompilerParams(vmem_limit_bytes=VMEM_LIMIT),
    )(x, mod_l, n2, yp, oc, os_, ow, gates, ge, wo, w1, w2)


def _rot_cols(w):
    shp = w.shape
    w4 = w.reshape(shp[:-1] + (shp[-1] // HEAD_DIM, 2, HEAD_DIM // 2))
    return jnp.stack([-w4[..., 1, :], w4[..., 0, :]], axis=-2).reshape(shp)


def _swap_halves(g):
    half = HEAD_DIM // 2
    return jnp.concatenate([g[..., half:], g[..., :half]], axis=-1)


def _rope_tables(pos, reps):
    half = HEAD_DIM // 2
    inv = ROPE_THETA ** (-jnp.arange(half, dtype=F32) * 2.0 / HEAD_DIM)
    ang = pos.astype(F32)[:, None] * inv[None, :]
    cos = jnp.tile(jnp.cos(ang), (1, 2 * reps))
    sin = jnp.tile(jnp.sin(ang), (1, 2 * reps))
    return cos, sin


def _overlap_t(n_chunks, n_sel):
    cs0 = np.arange(n_chunks) * CMP_STRIDE
    ss0 = np.arange(n_sel) * SEL_LEN
    ov = np.minimum(cs0[:, None] + CMP_LEN, ss0[None, :] + SEL_LEN) - np.maximum(cs0[:, None], ss0[None, :])
    ov = np.clip(ov, 0, None).astype(np.float32) / CMP_LEN
    out = np.zeros((2 * SEL_SLOTS, n_chunks), np.float32)
    out[SEL_SLOTS:SEL_SLOTS + n_sel] = ov.T
    return out


def _gate_expand():
    e = np.zeros((LANES, N_BRANCH * N_HEADS * HEAD_DIM), np.float32)
    for hh in range(N_HEADS):
        for br in range(N_BRANCH):
            c0 = br * N_HEADS * HEAD_DIM + hh * HEAD_DIM
            e[hh * N_BRANCH + br, c0:c0 + HEAD_DIM] = 1.0
    return e


def kernel(x, c, w_mod, b_mod, norm1, norm2, w_in, pool_w, pool_scale, q_norm, k_norm, cmp_pe, cmp_w1, cmp_w2, w_out, w_ff1, w_ff2):
    B, S, D = x.shape
    depth = w_mod.shape[0]
    n_chunks = S // CMP_STRIDE
    n_sel = S // SEL_LEN
    assert D == 1024 and S % 2048 == 0 and n_sel <= SEL_SLOTS
    topn = min(SEL_TOPN, n_sel)
    tm_in, tm_out, tm_pool, tq, tk = 256, 256, 512, 128, 512

    mod = _modulation(c, w_mod, b_mod)

    o0, o1, o2 = 512, 1024, 1024 + 768
    w_q = w_in[:, :, o0:o1]
    w_kv = w_in[:, :, o1:o2]
    w_g = jnp.pad(w_in[:, :, o2:], ((0, 0), (0, 0), (0, LANES - N_HEADS * N_BRANCH)))
    w_all = jnp.concatenate([w_in[:, :, :o0], w_q, _rot_cols(w_q), w_kv,
                             _rot_cols(w_kv[:, :, 128:384]), w_g], axis=-1).astype(BF16)
    gq = jnp.stack([jnp.tile(q_norm, (1, N_HEADS)), jnp.tile(_swap_halves(q_norm), (1, N_HEADS))], axis=1)
    gk = jnp.stack([jnp.tile(k_norm[:, 1], (1, 2)), jnp.tile(_swap_halves(k_norm[:, 1]), (1, 2)),
                    jnp.tile(k_norm[:, 2], (1, 2)), jnp.tile(_swap_halves(k_norm[:, 2]), (1, 2))], axis=1)
    pad64 = ((0, 0), (0, HEAD_DIM))
    kn = jnp.stack([jnp.pad(k_norm[:, 0], pad64), jnp.pad(_swap_halves(k_norm[:, 0]), pad64)], axis=1)
    padw = ((0, 0), (0, 0), (0, HEAD_DIM))
    w2k = jnp.concatenate([jnp.pad(cmp_w2[:, 0], padw), jnp.pad(_rot_cols(cmp_w2[:, 0]), padw)],
                          axis=-1).astype(BF16)
    w2v = jnp.pad(cmp_w2[:, 1], padw).astype(BF16)
    pe = jnp.broadcast_to(cmp_pe.reshape(depth, 2, 1, CMP_LEN * HEAD_DIM),
                          (depth, 2, 8, CMP_LEN * HEAD_DIM))
    pool_wb = pool_w.astype(BF16)
    wo_b, w1_b, w2_b = w_out.astype(BF16), w_ff1.astype(BF16), w_ff2.astype(BF16)

    cos, sin = _rope_tables(jnp.arange(S), 2)
    cos_c, sin_c = _rope_tables(jnp.arange(n_chunks) * CMP_STRIDE + (CMP_LEN - 1), 2)
    bd = jnp.asarray(np.kron(np.eye(N_HEADS, dtype=np.float32),
                             np.full((HEAD_DIM, HEAD_DIM), 1.0 / HEAD_DIM, np.float32)), BF16)
    ovt = jnp.asarray(_overlap_t(n_chunks, n_sel), BF16)
    ge = jnp.asarray(_gate_expand(), BF16)

    for l in range(depth):
        u, q, ks, kw, vs, vw, kc, vc, gates = _inproj(
            x, mod, l, norm1[l:l + 1], w_all[l], cos, sin, gq[l], gk[l], bd, tm_in)
        width = CMP_STRIDE * HEAD_DIM
        kcmp, vcmp = _compress(kc.reshape(B * KV_GROUPS, n_chunks, width),
                               vc.reshape(B * KV_GROUPS, n_chunks, width),
                               cmp_w1[l], pe[l], w2k[l], w2v[l], kn[l], cos_c, sin_c)
        o_cmp, qs = _cmp_attn(q, kcmp, vcmp, ovt, tq, topn)
        o_sel = _sel_attn(qs, ks, vs, tq, tk)
        o_win = _win_attn(qs, kw, vw, tq)
        y_pool = _pool(u, pool_wb[l], pool_scale[l:l + 1], tm_pool)
        x = _out_ffn(x, mod, l, norm2[l:l + 1], y_pool, o_cmp, o_sel, o_win, gates, ge,
                     wo_b[l], w1_b[l], w2_b[l], tm_out)
    return x
```

```python
import functools
import math

import numpy as np
import jax
import jax.numpy as jnp
from jax import lax
from jax.experimental import pallas as pl
from jax.experimental.pallas import tpu as pltpu

F32 = jnp.float32
BF16 = jnp.bfloat16

LANES = 128
SUBLANES = 8
HEAD_DIM = 64
N_HEADS = 8
KV_GROUPS = 2
HEADS_PER_GROUP = N_HEADS // KV_GROUPS
N_BRANCH = 3
GATE_ROWS = 16
POOL_WINDOWS = (2, 4, 8, 16)
POOL_HALO = 16
CMP_LEN = 32
CMP_STRIDE = 16
SEL_LEN = 64
SEL_SLOTS = 64
SEL_TOPN = 16
SUB_Q = 128
WINDOW = 512
ROPE_THETA = 10000.0
EPS = 1e-6
NEG = -1e30
FORCED_SCORE = 1e4
LOG2E = math.log2(math.e)
SCALE = 1.0 / math.sqrt(HEAD_DIM)
VMEM_LIMIT = 56 * 1024 * 1024
MXU_COLS = 256


def _dot(a, b):
    return jnp.dot(a, b, preferred_element_type=F32)


def _split_bf16(v):
    hi = v.astype(BF16)
    lo = (v - hi.astype(F32)).astype(BF16)
    return hi, lo


def _mod_kernel(c_ref, w_ref, b_ref, o_ref, *, batch):
    c = c_ref[...]
    rows = c.shape[0]
    ca = c * jax.nn.sigmoid(c)
    hi = ca.astype(BF16).astype(F32)
    row = lax.broadcasted_iota(jnp.int32, ca.shape, 0)
    a = jnp.where(row < batch, hi, ca - hi).astype(BF16)
    w = w_ref[0]
    w_hi, w_lo = _split_bf16(w)
    r1 = _dot(a, w_hi)
    r2 = _dot(a, w_lo)
    o_ref[0] = r1 + pltpu.roll(r1, rows - batch, axis=0) + r2 + b_ref[0]


def _modulation(c, w_mod, b_mod):
    depth, d, n = w_mod.shape
    batch = c.shape[0]
    rows = -(-2 * batch // SUBLANES) * SUBLANES
    c2 = jnp.zeros((rows, d), F32).at[:batch].set(c).at[batch:2 * batch].set(c)
    tn = 1536
    out = pl.pallas_call(
        functools.partial(_mod_kernel, batch=batch),
        grid=(depth, n // tn),
        in_specs=[pl.BlockSpec((rows, d), lambda l, j: (0, 0)),
                  pl.BlockSpec((1, d, tn), lambda l, j: (l, 0, j)),
                  pl.BlockSpec((1, 1, tn), lambda l, j: (l, 0, j))],
        out_specs=pl.BlockSpec((1, rows, tn), lambda l, j: (l, 0, j)),
        out_shape=jax.ShapeDtypeStruct((depth, rows, n), F32),
        compiler_params=pltpu.CompilerParams(vmem_limit_bytes=VMEM_LIMIT),
    )(c2, w_mod, b_mod.reshape(depth, 1, n))
    return out[:, :batch].reshape(depth, batch, 6, d)


C_U = 0
C_Q = 512
C_QR = 1024
C_KV = 1536
C_KR = 2304
C_G = 2560
C_END = 2688


def _inproj_kernel(x_ref, mod_ref, n1_ref, w_ref, cos_ref, sin_ref, gq_ref, gk_ref, bd_ref,
                   u_ref, qt_ref, ks_ref, kw_ref, vst_ref, vwt_ref, kc_ref, vc_ref, gt_ref, *, tm):
    i = pl.program_id(1)
    x = x_ref[0]
    shift = mod_ref[0, 0, 0:1, :]
    scale = mod_ref[0, 0, 1:2, :]
    ms = jnp.mean(x * x, axis=-1, keepdims=True)
    h = (x * lax.rsqrt(ms + EPS) * n1_ref[...]) * (1.0 + scale) + shift
    hb = h.astype(BF16)

    def proj(a, b):
        return _dot(hb, w_ref[:, a:b])

    def seg_mean(v, bdm):
        hi, lo = _split_bf16(v * v)
        return _dot(hi, bdm) + _dot(lo, bdm)

    u_ref[0] = proj(C_U, C_Q)

    cos = cos_ref[...]
    sin = sin_ref[...]
    cos4 = jnp.concatenate([cos] * 4, axis=1)
    sin4 = jnp.concatenate([sin] * 4, axis=1)
    q = proj(C_Q, C_QR)
    qr = proj(C_QR, C_KV)
    rq = lax.rsqrt(seg_mean(q, bd_ref[...]) + EPS)
    qo = (q * gq_ref[0:1, :] * cos4 + qr * gq_ref[1:2, :] * sin4) * (rq * (SCALE * LOG2E))
    for hp in range(N_HEADS // 2):
        qt_ref[0, hp] = qo[:, LANES * hp:LANES * (hp + 1)].T.astype(BF16)

    kv = proj(C_KV, C_KR)
    kr = proj(C_KR, C_G)
    bd2 = bd_ref[0:LANES, 0:LANES]
    k_c, k_s, k_w = kv[:, 0:128], kv[:, 128:256], kv[:, 256:384]
    v_c, v_s, v_w = kv[:, 384:512], kv[:, 512:640], kv[:, 640:768]
    rs = lax.rsqrt(seg_mean(k_s, bd2) + EPS)
    kso = (k_s * gk_ref[0:1, :] * cos + kr[:, 0:128] * gk_ref[1:2, :] * sin) * rs
    rw = lax.rsqrt(seg_mean(k_w, bd2) + EPS)
    kwo = (k_w * gk_ref[2:3, :] * cos + kr[:, 128:256] * gk_ref[3:4, :] * sin) * rw

    lane = lax.broadcasted_iota(jnp.int32, (tm, LANES), 1)
    low = lane < HEAD_DIM
    pos = i * tm + lax.broadcasted_iota(jnp.int32, (tm, LANES), 0)
    blk = pos // SEL_LEN
    blk_flag = jnp.where(lane - HEAD_DIM == blk, -NEG, 0.0)
    ones_col = jnp.where(lane == HEAD_DIM, 1.0, 0.0)
    for g in range(KV_GROUPS):
        def grp(a, g=g):
            return pltpu.roll(a, HEAD_DIM, axis=1) if g else a
        ks_ref[0, g] = jnp.where(low, grp(kso), blk_flag).astype(BF16)
        kw_ref[0, g] = jnp.where(low, grp(kwo), 0.0).astype(BF16)
        vst_ref[0, g] = jnp.where(low, grp(v_s), ones_col).T.astype(BF16)
        vwt_ref[0, g] = jnp.where(low, grp(v_w), ones_col).T.astype(BF16)
        kc_ref[0, g] = grp(k_c)[:, 0:HEAD_DIM].astype(BF16)
        vc_ref[0, g] = grp(v_c)[:, 0:HEAD_DIM].astype(BF16)
    gt_ref[0] = jax.nn.sigmoid(proj(C_G, C_END)).T[0:KV_GROUPS * GATE_ROWS]


def _inproj(x, mod_l, layer, n1, w, cos, sin, gq, gk, bd, tm):
    B, S, D = x.shape
    kern = functools.partial(_inproj_kernel, tm=tm)
    grp_rows = lambda width: pl.BlockSpec((1, KV_GROUPS, tm, width), lambda b, i: (b, 0, i, 0))
    grp_cols = pl.BlockSpec((1, KV_GROUPS, LANES, tm), lambda b, i: (b, 0, 0, i))
    rows_shape = lambda width: jax.ShapeDtypeStruct((B, KV_GROUPS, S, width), BF16)
    cols_shape = jax.ShapeDtypeStruct((B, KV_GROUPS, LANES, S), BF16)
    const = lambda shape: pl.BlockSpec(shape, lambda b, i: (0,) * len(shape))
    return pl.pallas_call(
        kern,
        grid=(B, S // tm),
        in_specs=[pl.BlockSpec((1, tm, D), lambda b, i: (b, i, 0)),
                  pl.BlockSpec((1, 1, 6, D), lambda b, i: (layer, b, 0, 0)),
                  const((1, D)),
                  const((D, C_END)),
                  pl.BlockSpec((tm, LANES), lambda b, i: (i, 0)),
                  pl.BlockSpec((tm, LANES), lambda b, i: (i, 0)),
                  const((2, 512)),
                  const((4, LANES)),
                  const((512, 512))],
        out_specs=[pl.BlockSpec((1, tm, 512), lambda b, i: (b, i, 0)),
                   pl.BlockSpec((1, N_HEADS // 2, LANES, tm), lambda b, i: (b, 0, 0, i)),
                   grp_rows(LANES), grp_rows(LANES), grp_cols, grp_cols,
                   grp_rows(HEAD_DIM), grp_rows(HEAD_DIM),
                   pl.BlockSpec((1, KV_GROUPS * GATE_ROWS, tm), lambda b, i: (b, 0, i))],
        out_shape=[jax.ShapeDtypeStruct((B, S, 512), F32),
                   jax.ShapeDtypeStruct((B, N_HEADS // 2, LANES, S), BF16),
                   rows_shape(LANES), rows_shape(LANES), cols_shape, cols_shape,
                   rows_shape(HEAD_DIM), rows_shape(HEAD_DIM),
                   jax.ShapeDtypeStruct((B, KV_GROUPS * GATE_ROWS, S), F32)],
        compiler_params=pltpu.CompilerParams(vmem_limit_bytes=VMEM_LIMIT),
    )(x, mod_l, n1, w, cos, sin, gq, gk, bd)


def _compress_kernel(xk_ref, xv_ref, w1_ref, pe_ref, w2k_ref, w2v_ref, kn_ref, cos_ref, sin_ref,
                     ko_ref, vot_ref):
    nc = xk_ref.shape[1]
    half = (CMP_LEN // 2) * HEAD_DIM

    def hidden(x, w1, pe):
        w1_hi, w1_lo = _split_bf16(w1)
        pe_hi, pe_lo = _split_bf16(pe)
        bias = _dot(pe_hi, w1_hi) + _dot(pe_lo, w1_hi) + _dot(pe_hi, w1_lo)
        a = _dot(x, w1_hi[0:half])
        b = _dot(x, w1_hi[half:2 * half])
        pre = a + pltpu.roll(b, nc - 1, axis=0) + bias[0:1, :]
        return (pre * jax.nn.sigmoid(pre)).astype(BF16)

    hk = hidden(xk_ref[0], w1_ref[0], pe_ref[0])
    kk = _dot(hk, w2k_ref[...])
    k = kk[:, 0:LANES]
    kr = kk[:, LANES:2 * LANES]
    r = lax.rsqrt(jnp.sum(k * k, axis=-1, keepdims=True) * (1.0 / HEAD_DIM) + EPS)
    ko = (k * kn_ref[0:1, :] * cos_ref[...] + kr * kn_ref[1:2, :] * sin_ref[...]) * r
    ko_ref[0] = ko.astype(BF16)
    hv = hidden(xv_ref[0], w1_ref[1], pe_ref[1])
    vot_ref[0] = _dot(hv, w2v_ref[...]).T.astype(BF16)


def _compress(xk, xv, w1, pe, w2k, w2v, kn, cos_c, sin_c):
    n, nc, width = xk.shape
    const = lambda shape: pl.BlockSpec(shape, lambda j: (0,) * len(shape))
    return pl.pallas_call(
        _compress_kernel,
        grid=(n,),
        in_specs=[pl.BlockSpec((1, nc, width), lambda j: (j, 0, 0)),
                  pl.BlockSpec((1, nc, width), lambda j: (j, 0, 0)),
                  const(w1.shape), const(pe.shape), const(w2k.shape), const(w2v.shape),
                  const(kn.shape), const(cos_c.shape), const(sin_c.shape)],
        out_specs=[pl.BlockSpec((1, nc, LANES), lambda j: (j, 0, 0)),
                   pl.BlockSpec((1, LANES, nc), lambda j: (j, 0, 0))],
        out_shape=[jax.ShapeDtypeStruct((n, nc, LANES), BF16),
                   jax.ShapeDtypeStruct((n, LANES, nc), BF16)],
        compiler_params=pltpu.CompilerParams(vmem_limit_bytes=VMEM_LIMIT),
    )(xk, xv, w1, pe, w2k, w2v, kn, cos_c, sin_c)


def _nsa_kernel(qt_ref, kc_ref, vct_ref, ov_ref, ks_ref, vst_ref, kw_ref, vwt_ref, gt_ref,
                y_ref, q_sc, acc_sc, s_sc, p_sc, *, tq, tk, topn):
    i = pl.program_id(2)
    nsub = tq // SUB_Q
    wsub = HEADS_PER_GROUP * SUB_Q
    cols = nsub * wsub
    nc = kc_ref.shape[1]

    def lanes(jq, hh):
        return slice(jq * wsub + hh * SUB_Q, jq * wsub + (hh + 1) * SUB_Q)

    def toks(jq):
        return slice(jq * SUB_Q, (jq + 1) * SUB_Q)

    for jq in range(nsub):
        for hh in range(HEADS_PER_GROUP):
            q_sc[0:HEAD_DIM, lanes(jq, hh)] = qt_ref[
                0, hh // 2, (hh % 2) * HEAD_DIM:(hh % 2 + 1) * HEAD_DIM, toks(jq)]
    q_sc[HEAD_DIM:LANES, :] = jnp.zeros((LANES - HEAD_DIM, cols), BF16)

    def tok(shape, lane0=0):
        lane = lane0 + lax.broadcasted_iota(jnp.int32, shape, 1)
        return i * tq + (lane // wsub) * SUB_Q + (lane & (SUB_Q - 1))

    s = _dot(kc_ref[0], q_sc[...])
    nwin = WINDOW // SUB_Q
    win_scores, win_starts = [], []
    for jq in range(nsub):
        q_sub = q_sc[:, jq * wsub:(jq + 1) * wsub]
        for c in range(nwin + 1):
            ci = i * nsub + jq - nwin + c
            start = pl.multiple_of(jnp.maximum(ci, 0) * SUB_Q, SUB_Q)
            win_scores.append(_dot(kw_ref[0, 0, pl.ds(start, SUB_Q), :], q_sub))
            win_starts.append(start)

    n = lax.broadcasted_iota(jnp.int32, (nc, cols), 0)
    mask = n * CMP_STRIDE + (CMP_LEN - 1) <= tok((nc, cols))
    s = jnp.where(mask, s, NEG)
    m = jnp.max(s, axis=0, keepdims=True)
    p = jnp.where(mask, jnp.exp2(s - m), 0.0)
    l = jnp.sum(p, axis=0, keepdims=True)
    p = p * (1.0 / jnp.where(l > 0.0, l, 1.0))
    o_cmp = _dot(vct_ref[0], p.astype(BF16))

    sums = []
    for jq in range(nsub):
        ps = p[:, lanes(jq, 0)]
        for hh in range(1, HEADS_PER_GROUP):
            ps = ps + p[:, lanes(jq, hh)]
        sums.append(ps)
    hi, lo = _split_bf16(jnp.concatenate(sums, axis=1))
    ov = ov_ref[...]
    imp = _dot(ov, hi) + _dot(ov, lo)

    kk = lax.broadcasted_iota(jnp.int32, (SUB_Q, wsub), 0)
    r = lax.broadcasted_iota(jnp.int32, (SUB_Q, wsub), 1) & (SUB_Q - 1)
    o_win = []
    for jq in range(nsub):
        masked = []
        for c in range(nwin + 1):
            ci = i * nsub + jq - nwin + c
            sw = win_scores[jq * (nwin + 1) + c]
            if c == nwin:
                sw = jnp.where(kk <= r, sw, NEG)
            else:
                if c == 0:
                    sw = jnp.where(kk > r, sw, NEG)
                sw = sw + jnp.where(ci >= 0, 0.0, NEG)
            masked.append(sw)
        mw = masked[0].max(axis=0, keepdims=True)
        for sw in masked[1:]:
            mw = jnp.maximum(mw, sw.max(axis=0, keepdims=True))
        ow = jnp.zeros((LANES, wsub), F32)
        for c, sw in enumerate(masked):
            pw = jnp.exp2(sw - mw).astype(BF16)
            ow = ow + _dot(vwt_ref[0, 0, :, pl.ds(win_starts[jq * (nwin + 1) + c], SUB_Q)], pw)
        o_win.append(ow)

    j = lax.broadcasted_iota(jnp.int32, (SEL_SLOTS, tq), 0)
    cur = (i * tq + lax.broadcasted_iota(jnp.int32, (SEL_SLOTS, tq), 1)) // SEL_LEN
    forced = (j == 0) | (j == cur) | (j == cur - 1)
    imp = jnp.where(j <= cur, jnp.where(forced, FORCED_SCORE, imp), -1.0)
    sub = SUBLANES
    slabs = [imp[a:a + sub] for a in range(0, SEL_SLOTS, sub)]
    ranks = [jnp.zeros((sub, tq), F32) for _ in slabs]
    sub_row = lax.broadcasted_iota(jnp.int32, (sub, tq), 0)
    for jj in range(SEL_SLOTS):
        rowj = jnp.broadcast_to(imp[jj:jj + 1, :], (sub, tq))
        for a, xs in enumerate(slabs):
            if a * sub > jj:
                inc = jnp.where(rowj >= xs, 1.0, 0.0)
            elif a * sub + sub - 1 <= jj:
                inc = jnp.where(rowj > xs, 1.0, 0.0)
            else:
                tie = jnp.where(sub_row > jj - a * sub, 1.0, 0.0)
                inc = jnp.where(rowj > xs, 1.0, jnp.where(rowj == xs, tie, 0.0))
            ranks[a] = ranks[a] + inc
    rank = jnp.concatenate(ranks, axis=0)
    keep = (rank < float(topn)) & (imp >= 0.0)
    flag = jnp.where(keep, 0.0, -1.0).astype(BF16)

    for jq in range(nsub):
        for hh in range(HEADS_PER_GROUP):
            q_sc[HEAD_DIM:LANES, lanes(jq, hh)] = flag[:, toks(jq)]
    acc_sc[...] = jnp.zeros(acc_sc.shape, F32)

    blocks = [slice(a, a + MXU_COLS) for a in range(0, cols, MXU_COLS)]

    def score_stage(kt, slot):
        k = ks_ref[0, 0, pl.ds(pl.multiple_of(kt * tk, tk), tk), :]
        m_out = []
        for cs in blocks:
            ss = _dot(k, q_sc[:, cs])
            s_sc[slot, :, cs] = ss
            m_out.append(jnp.max(ss, axis=0, keepdims=True))
        return jnp.concatenate(m_out, axis=1)

    def softmax_stage(kt, slot, m_run, m_tile, diagonal):
        m_out, a_out = [], []
        for cs in blocks:
            ss = s_sc[slot, :, cs]
            m_blk = m_tile[:, cs]
            if diagonal:
                key = kt * tk + lax.broadcasted_iota(jnp.int32, (tk, MXU_COLS), 0)
                ss = jnp.where(key <= tok((tk, MXU_COLS), cs.start), ss, NEG)
                m_blk = jnp.max(ss, axis=0, keepdims=True)
            m_new = jnp.maximum(m_run[:, cs], m_blk)
            p_sc[slot, :, cs] = jnp.exp2(ss - m_new).astype(BF16)
            m_out.append(m_new)
            a_out.append(jnp.exp2(m_run[:, cs] - m_new))
        return jnp.concatenate(m_out, axis=1), jnp.concatenate(a_out, axis=1)

    def value_stage(kt, slot, alpha):
        vt = vst_ref[0, 0, :, pl.ds(pl.multiple_of(jnp.maximum(kt, 0) * tk, tk), tk)]
        for cs in blocks:
            acc_sc[:, cs] = acc_sc[:, cs] * alpha[:, cs] + _dot(vt, p_sc[slot, :, cs])

    n_tiles = (i * tq) // tk + 1
    m_first = score_stage(0, 0)
    p_sc[1] = jnp.zeros(p_sc.shape[1:], BF16)

    def steady(k, carry):
        m_run, alpha, m_tile = carry

        def arm(slot):
            m_new, a_new = softmax_stage(k - 1, 1 - slot, m_run, m_tile, False)
            m_next = score_stage(k, slot)
            value_stage(k - 2, slot, alpha)
            return m_new, a_new, m_next

        return lax.cond(k % 2 == 0, lambda: arm(0), lambda: arm(1))

    m_run, alpha, m_tile = lax.fori_loop(
        1, n_tiles, steady,
        (jnp.full((1, cols), NEG, F32), jnp.ones((1, cols), F32), m_first))

    def drain(slot):
        value_stage(n_tiles - 2, 1 - slot, alpha)
        _, a_last = softmax_stage(n_tiles - 1, slot, m_run, m_tile, True)
        value_stage(n_tiles - 1, slot, a_last)

    @pl.when((n_tiles - 1) % 2 == 0)
    def _():
        drain(0)

    @pl.when((n_tiles - 1) % 2 == 1)
    def _():
        drain(1)

    o_sel = acc_sc[...]

    gt = gt_ref[0]
    tiles = []
    for jq in range(nsub):
        heads = []
        for hh in range(HEADS_PER_GROUP):
            cs = lanes(jq, hh)
            ws = slice(hh * SUB_Q, (hh + 1) * SUB_Q)
            g_c = gt[hh:hh + 1, toks(jq)]
            g_s = gt[HEADS_PER_GROUP + hh:HEADS_PER_GROUP + hh + 1, toks(jq)]
            g_w = gt[2 * HEADS_PER_GROUP + hh:2 * HEADS_PER_GROUP + hh + 1, toks(jq)]
            ow = o_win[jq]
            heads.append(o_cmp[0:HEAD_DIM, cs] * g_c
                         + o_sel[0:HEAD_DIM, cs] * (g_s / o_sel[HEAD_DIM:HEAD_DIM + 1, cs])
                         + ow[0:HEAD_DIM, ws] * (g_w / ow[HEAD_DIM:HEAD_DIM + 1, ws]))
        tiles.append(jnp.concatenate(heads, axis=0).T)
    y_ref[0] = jnp.concatenate(tiles, axis=0).astype(BF16)


def _nsa(qt, kcmp, vcmpt, ov, ks, vst, kw, vwt, gt, tq, tk, topn):
    B, _, _, S = qt.shape
    nc = kcmp.shape[1]
    cols = HEADS_PER_GROUP * tq
    kern = functools.partial(_nsa_kernel, tq=tq, tk=tk, topn=topn)
    seq_rows = pl.BlockSpec((1, 1, S, LANES), lambda b, g, i: (b, g, 0, 0))
    seq_cols = pl.BlockSpec((1, 1, LANES, S), lambda b, g, i: (b, g, 0, 0))
    return pl.pallas_call(
        kern,
        grid=(B, KV_GROUPS, S // tq),
        in_specs=[pl.BlockSpec((1, HEADS_PER_GROUP // 2, LANES, tq), lambda b, g, i: (b, g, 0, i)),
                  pl.BlockSpec((1, nc, LANES), lambda b, g, i: (b * KV_GROUPS + g, 0, 0)),
                  pl.BlockSpec((1, LANES, nc), lambda b, g, i: (b * KV_GROUPS + g, 0, 0)),
                  pl.BlockSpec((SEL_SLOTS, nc), lambda b, g, i: (0, 0)),
                  seq_rows, seq_cols, seq_rows, seq_cols,
                  pl.BlockSpec((1, GATE_ROWS, tq), lambda b, g, i: (b, g, i))],
        out_specs=pl.BlockSpec((1, tq, HEADS_PER_GROUP * HEAD_DIM), lambda b, g, i: (b, i, g)),
        out_shape=jax.ShapeDtypeStruct((B, S, N_HEADS * HEAD_DIM), BF16),
        scratch_shapes=[pltpu.VMEM((LANES, cols), BF16), pltpu.VMEM((LANES, cols), F32),
                        pltpu.VMEM((2, tk, cols), F32), pltpu.VMEM((2, tk, cols), BF16)],
        compiler_params=pltpu.CompilerParams(vmem_limit_bytes=VMEM_LIMIT),
    )(qt, kcmp, vcmpt, ov, ks, vst, kw, vwt, gt)


def _pool_kernel(u_ref, up_ref, pw_ref, psc_ref, y_ref, ext_sc, *, tm):
    i = pl.program_id(1)
    ext_sc[0:POOL_HALO, :] = jnp.where(i > 0, up_ref[0], 0.0)
    ext_sc[POOL_HALO:POOL_HALO + tm, :] = u_ref[0]
    pos = i * tm + lax.broadcasted_iota(jnp.int32, (tm, LANES), 0)
    for g, w in enumerate(POOL_WINDOWS):
        c0, c1 = g * LANES, (g + 1) * LANES
        cur = ext_sc[POOL_HALO:POOL_HALO + tm, c0:c1]
        acc = cur
        for k in range(1, w):
            acc = acc + ext_sc[POOL_HALO - k:POOL_HALO - k + tm, c0:c1]
        cnt = jnp.minimum(pos + 1, w).astype(F32)
        pooled = acc / cnt - cur
        y = _dot(pooled.astype(BF16), pw_ref[g]) * psc_ref[:, c0:c1]
        y_ref[0, :, c0:c1] = y.astype(BF16)


def _pool(u, pw, psc, tm):
    B, S, width = u.shape
    kern = functools.partial(_pool_kernel, tm=tm)
    per = tm // POOL_HALO
    return pl.pallas_call(
        kern,
        grid=(B, S // tm),
        in_specs=[pl.BlockSpec((1, tm, width), lambda b, i: (b, i, 0)),
                  pl.BlockSpec((1, POOL_HALO, width), lambda b, i: (b, jnp.maximum(i * per - 1, 0), 0)),
                  pl.BlockSpec(pw.shape, lambda b, i: (0, 0, 0)),
                  pl.BlockSpec((1, width), lambda b, i: (0, 0))],
        out_specs=pl.BlockSpec((1, tm, width), lambda b, i: (b, i, 0)),
        out_shape=jax.ShapeDtypeStruct((B, S, width), BF16),
        scratch_shapes=[pltpu.VMEM((tm + POOL_HALO, width), F32)],
        compiler_params=pltpu.CompilerParams(vmem_limit_bytes=VMEM_LIMIT),
    )(u, u, pw, psc)


def _out_ffn_kernel(x_ref, mod_ref, n2_ref, yp_ref, yn_ref, wo_ref, w1_ref, w2_ref, o_ref, *, ff_chunk):
    x = x_ref[0]
    gate1 = mod_ref[0, 0, 2:3, :]
    shift2 = mod_ref[0, 0, 3:4, :]
    scale2 = mod_ref[0, 0, 4:5, :]
    gate2 = mod_ref[0, 0, 5:6, :]
    half = wo_ref.shape[0] // 2
    mix = _dot(yp_ref[0], wo_ref[0:half, :]) + _dot(yn_ref[0], wo_ref[half:2 * half, :])
    x1 = x + gate1 * mix
    ms = jnp.mean(x1 * x1, axis=-1, keepdims=True)
    h2 = ((x1 * lax.rsqrt(ms + EPS) * n2_ref[...]) * (1.0 + scale2) + shift2).astype(BF16)
    d_ff = w1_ref.shape[1]
    ff = jnp.zeros(x.shape, F32)
    for c in range(d_ff // ff_chunk):
        a = jnp.maximum(_dot(h2, w1_ref[:, c * ff_chunk:(c + 1) * ff_chunk]), 0.0)
        ff = ff + _dot((a * a).astype(BF16), w2_ref[c * ff_chunk:(c + 1) * ff_chunk, :])
    o_ref[0] = x1 + gate2 * ff


def _out_ffn(x, mod_l, layer, n2, yp, yn, wo, w1, w2, tm):
    B, S, D = x.shape
    kern = functools.partial(_out_ffn_kernel, ff_chunk=1024)
    tok = lambda width: pl.BlockSpec((1, tm, width), lambda b, i: (b, i, 0))
    const = lambda shape: pl.BlockSpec(shape, lambda b, i: (0,) * len(shape),
                                       pipeline_mode=pl.Buffered(1))
    return pl.pallas_call(
        kern,
        grid=(B, S // tm),
        in_specs=[tok(D),
                  pl.BlockSpec((1, 1, 6, D), lambda b, i: (layer, b, 0, 0)),
                  const((1, D)),
                  tok(512), tok(512),
                  const(wo.shape), const(w1.shape), const(w2.shape)],
        out_specs=tok(D),
        out_shape=jax.ShapeDtypeStruct((B, S, D), F32),
        compiler_params=pltpu.CompilerParams(vmem_limit_bytes=VMEM_LIMIT),
    )(x, mod_l, n2, yp, yn, wo, w1, w2)


def _rot_cols(w):
    shp = w.shape
    w4 = w.reshape(shp[:-1] + (shp[-1] // HEAD_DIM, 2, HEAD_DIM // 2))
    return jnp.stack([-w4[..., 1, :], w4[..., 0, :]], axis=-2).reshape(shp)


def _swap_halves(g):
    half = HEAD_DIM // 2
    return jnp.concatenate([g[..., half:], g[..., :half]], axis=-1)


def _rope_tables(pos, reps):
    half = HEAD_DIM // 2
    inv = ROPE_THETA ** (-jnp.arange(half, dtype=F32) * 2.0 / HEAD_DIM)
    ang = pos.astype(F32)[:, None] * inv[None, :]
    cos = jnp.tile(jnp.cos(ang), (1, 2 * reps))
    sin = jnp.tile(jnp.sin(ang), (1, 2 * reps))
    return cos, sin


def _overlap(n_chunks, n_sel):
    cs0 = np.arange(n_chunks) * CMP_STRIDE
    ss0 = np.arange(n_sel) * SEL_LEN
    ov = np.minimum(cs0[:, None] + CMP_LEN, ss0[None, :] + SEL_LEN) - np.maximum(cs0[:, None], ss0[None, :])
    ov = np.clip(ov, 0, None).astype(np.float32) / CMP_LEN
    out = np.zeros((SEL_SLOTS, n_chunks), np.float32)
    out[:n_sel] = ov.T
    return out


def _gate_cols(w_g):
    idx = np.zeros((LANES,), np.int32)
    used = np.zeros((LANES,), np.float32)
    for g in range(KV_GROUPS):
        for br in range(N_BRANCH):
            for hg in range(HEADS_PER_GROUP):
                c = g * GATE_ROWS + br * HEADS_PER_GROUP + hg
                idx[c] = (g * HEADS_PER_GROUP + hg) * N_BRANCH + br
                used[c] = 1.0
    return w_g[..., idx] * used


def kernel(x, c, w_mod, b_mod, norm1, norm2, w_in, pool_w, pool_scale, q_norm, k_norm, cmp_pe, cmp_w1, cmp_w2, w_out, w_ff1, w_ff2):
    B, S, D = x.shape
    depth = w_mod.shape[0]
    n_chunks = S // CMP_STRIDE
    n_sel = S // SEL_LEN
    assert D == 1024 and S % 2048 == 0 and n_sel <= SEL_SLOTS
    topn = min(SEL_TOPN, n_sel)
    tm_in, tm_out, tm_pool, tq, tk = 256, 256, 512, 256, 512

    mod = _modulation(c, w_mod, b_mod)

    o0, o1, o2 = 512, 1024, 1024 + 768
    w_q = w_in[:, :, o0:o1]
    w_kv = w_in[:, :, o1:o2]
    w_all = jnp.concatenate([w_in[:, :, :o0], w_q, _rot_cols(w_q), w_kv,
                             _rot_cols(w_kv[:, :, 128:384]), _gate_cols(w_in[:, :, o2:])],
                            axis=-1).astype(BF16)
    gq = jnp.stack([jnp.tile(q_norm, (1, N_HEADS)), jnp.tile(_swap_halves(q_norm), (1, N_HEADS))], axis=1)
    gk = jnp.stack([jnp.tile(k_norm[:, 1], (1, 2)), jnp.tile(_swap_halves(k_norm[:, 1]), (1, 2)),
                    jnp.tile(k_norm[:, 2], (1, 2)), jnp.tile(_swap_halves(k_norm[:, 2]), (1, 2))], axis=1)
    pad64 = ((0, 0), (0, HEAD_DIM))
    kn = jnp.stack([jnp.pad(k_norm[:, 0], pad64), jnp.pad(_swap_halves(k_norm[:, 0]), pad64)], axis=1)
    padw = ((0, 0), (0, 0), (0, HEAD_DIM))
    w2k = jnp.concatenate([jnp.pad(cmp_w2[:, 0], padw), jnp.pad(_rot_cols(cmp_w2[:, 0]), padw)],
                          axis=-1).astype(BF16)
    w2v = jnp.pad(cmp_w2[:, 1], padw).astype(BF16)
    pe = jnp.broadcast_to(cmp_pe.reshape(depth, 2, 1, CMP_LEN * HEAD_DIM),
                          (depth, 2, SUBLANES, CMP_LEN * HEAD_DIM))
    pool_wb = pool_w.astype(BF16)
    wo_b, w1_b, w2_b = w_out.astype(BF16), w_ff1.astype(BF16), w_ff2.astype(BF16)

    cos, sin = _rope_tables(jnp.arange(S), 2)
    cos_c, sin_c = _rope_tables(jnp.arange(n_chunks) * CMP_STRIDE + (CMP_LEN - 1), 2)
    bd = jnp.asarray(np.kron(np.eye(N_HEADS, dtype=np.float32),
                             np.full((HEAD_DIM, HEAD_DIM), 1.0 / HEAD_DIM, np.float32)), BF16)
    ov = jnp.asarray(_overlap(n_chunks, n_sel), BF16)

    for l in range(depth):
        u, qt, ks, kw, vst, vwt, kc, vc, gt = _inproj(
            x, mod, l, norm1[l:l + 1], w_all[l], cos, sin, gq[l], gk[l], bd, tm_in)
        width = CMP_STRIDE * HEAD_DIM
        kcmp, vcmpt = _compress(kc.reshape(B * KV_GROUPS, n_chunks, width),
                                vc.reshape(B * KV_GROUPS, n_chunks, width),
                                cmp_w1[l], pe[l], w2k[l], w2v[l], kn[l], cos_c, sin_c)
        y_nsa = _nsa(qt, kcmp, vcmpt, ov, ks, vst, kw, vwt, gt, tq, tk, topn)
        y_pool = _pool(u, pool_wb[l], pool_scale[l:l + 1], tm_pool)
        x = _out_ffn(x, mod, l, norm2[l:l + 1], y_pool, y_nsa, wo_b[l], w1_b[l], w2_b[l], tm_out)
    return x
```

```python
import functools
import math

import numpy as np
import jax
import jax.numpy as jnp
from jax import lax
from jax.experimental import pallas as pl
from jax.experimental.pallas import tpu as pltpu

F32 = jnp.float32
BF16 = jnp.bfloat16

LANES = 128
SUBLANES = 8
HEAD_DIM = 64
N_HEADS = 8
KV_GROUPS = 2
HEADS_PER_GROUP = N_HEADS // KV_GROUPS
N_BRANCH = 3
GATE_ROWS = 16
POOL_WINDOWS = (2, 4, 8, 16)
POOL_HALO = 16
CMP_LEN = 32
CMP_STRIDE = 16
SEL_LEN = 64
SEL_SLOTS = 64
SEL_TOPN = 16
SUB_Q = 128
WINDOW = 512
ROPE_THETA = 10000.0
EPS = 1e-6
NEG = -1e30
FORCED_SCORE = 1e4
LOG2E = math.log2(math.e)
SCALE = 1.0 / math.sqrt(HEAD_DIM)
VMEM_LIMIT = 56 * 1024 * 1024
MXU_COLS = 256


def _dot(a, b):
    return jnp.dot(a, b, preferred_element_type=F32)


def _split_bf16(v):
    hi = v.astype(BF16)
    lo = (v - hi.astype(F32)).astype(BF16)
    return hi, lo


def _mod_kernel(c_ref, w_ref, b_ref, o_ref, *, batch):
    c = c_ref[...]
    rows = c.shape[0]
    ca = c * jax.nn.sigmoid(c)
    hi = ca.astype(BF16).astype(F32)
    row = lax.broadcasted_iota(jnp.int32, ca.shape, 0)
    a = jnp.where(row < batch, hi, ca - hi).astype(BF16)
    w = w_ref[0]
    w_hi, w_lo = _split_bf16(w)
    r1 = _dot(a, w_hi)
    r2 = _dot(a, w_lo)
    o_ref[0] = r1 + pltpu.roll(r1, rows - batch, axis=0) + r2 + b_ref[0]


def _modulation(c, w_mod, b_mod):
    depth, d, n = w_mod.shape
    batch = c.shape[0]
    rows = -(-2 * batch // SUBLANES) * SUBLANES
    c2 = jnp.zeros((rows, d), F32).at[:batch].set(c).at[batch:2 * batch].set(c)
    tn = 1536
    out = pl.pallas_call(
        functools.partial(_mod_kernel, batch=batch),
        grid=(depth, n // tn),
        in_specs=[pl.BlockSpec((rows, d), lambda l, j: (0, 0)),
                  pl.BlockSpec((1, d, tn), lambda l, j: (l, 0, j)),
                  pl.BlockSpec((1, 1, tn), lambda l, j: (l, 0, j))],
        out_specs=pl.BlockSpec((1, rows, tn), lambda l, j: (l, 0, j)),
        out_shape=jax.ShapeDtypeStruct((depth, rows, n), F32),
        compiler_params=pltpu.CompilerParams(vmem_limit_bytes=VMEM_LIMIT),
    )(c2, w_mod, b_mod.reshape(depth, 1, n))
    return out[:, :batch].reshape(depth, batch, 6, d)


C_U = 0
C_Q = 512
C_QR = 1024
C_KV = 1536
C_KR = 2304
C_G = 2560
C_END = 2688


def _inproj_kernel(x_ref, mod_ref, n1_ref, w_ref, cos_ref, sin_ref, gq_ref, gk_ref, bd_ref,
                   u_ref, qt_ref, ks_ref, kw_ref, vst_ref, vwt_ref, kvc_ref, gt_ref, *, tm):
    i = pl.program_id(1)
    x = x_ref[0]
    shift = mod_ref[0, 0, 0:1, :]
    scale = mod_ref[0, 0, 1:2, :]
    ms = jnp.mean(x * x, axis=-1, keepdims=True)
    h = (x * lax.rsqrt(ms + EPS) * n1_ref[...]) * (1.0 + scale) + shift
    hb = h.astype(BF16)

    def proj(a, b):
        return _dot(hb, w_ref[:, a:b])

    def seg_mean(v, bdm):
        hi, lo = _split_bf16(v * v)
        return _dot(hi, bdm) + _dot(lo, bdm)

    u_ref[0] = proj(C_U, C_Q)

    cos = cos_ref[...]
    sin = sin_ref[...]
    cos4 = jnp.concatenate([cos] * 4, axis=1)
    sin4 = jnp.concatenate([sin] * 4, axis=1)
    q = proj(C_Q, C_QR)
    qr = proj(C_QR, C_KV)
    rq = lax.rsqrt(seg_mean(q, bd_ref[...]) + EPS)
    qo = (q * gq_ref[0:1, :] * cos4 + qr * gq_ref[1:2, :] * sin4) * (rq * (SCALE * LOG2E))
    for hp in range(N_HEADS // 2):
        qt_ref[0, hp] = qo[:, LANES * hp:LANES * (hp + 1)].T.astype(BF16)

    kv = proj(C_KV, C_KR)
    kr = proj(C_KR, C_G)
    bd2 = bd_ref[0:LANES, 0:LANES]
    k_c, k_s, k_w = kv[:, 0:128], kv[:, 128:256], kv[:, 256:384]
    v_c, v_s, v_w = kv[:, 384:512], kv[:, 512:640], kv[:, 640:768]
    rs = lax.rsqrt(seg_mean(k_s, bd2) + EPS)
    kso = (k_s * gk_ref[0:1, :] * cos + kr[:, 0:128] * gk_ref[1:2, :] * sin) * rs
    rw = lax.rsqrt(seg_mean(k_w, bd2) + EPS)
    kwo = (k_w * gk_ref[2:3, :] * cos + kr[:, 128:256] * gk_ref[3:4, :] * sin) * rw

    lane = lax.broadcasted_iota(jnp.int32, (tm, LANES), 1)
    low = lane < HEAD_DIM
    pos = i * tm + lax.broadcasted_iota(jnp.int32, (tm, LANES), 0)
    blk = pos // SEL_LEN
    blk_flag = jnp.where(lane - HEAD_DIM == blk, -NEG, 0.0)
    ones_col = jnp.where(lane == HEAD_DIM, 1.0, 0.0)
    for g in range(KV_GROUPS):
        def grp(a, g=g):
            return pltpu.roll(a, HEAD_DIM, axis=1) if g else a
        ks_ref[0, g] = jnp.where(low, grp(kso), blk_flag).astype(BF16)
        kw_ref[0, g] = jnp.where(low, grp(kwo), 0.0).astype(BF16)
        vst_ref[0, g] = jnp.where(low, grp(v_s), ones_col).T.astype(BF16)
        vwt_ref[0, g] = jnp.where(low, grp(v_w), ones_col).T.astype(BF16)
        kvc_ref[0, g] = jnp.where(low, grp(k_c), v_c if g else pltpu.roll(v_c, HEAD_DIM, axis=1))
    gt_ref[0] = jax.nn.sigmoid(proj(C_G, C_END)).T[0:KV_GROUPS * GATE_ROWS]


def _inproj(x, mod_l, layer, n1, w, cos, sin, gq, gk, bd, tm):
    B, S, D = x.shape
    kern = functools.partial(_inproj_kernel, tm=tm)
    grp_rows = lambda width: pl.BlockSpec((1, KV_GROUPS, tm, width), lambda b, i: (b, 0, i, 0))
    grp_cols = pl.BlockSpec((1, KV_GROUPS, LANES, tm), lambda b, i: (b, 0, 0, i))
    rows_shape = lambda width: jax.ShapeDtypeStruct((B, KV_GROUPS, S, width), BF16)
    cols_shape = jax.ShapeDtypeStruct((B, KV_GROUPS, LANES, S), BF16)
    const = lambda shape: pl.BlockSpec(shape, lambda b, i: (0,) * len(shape))
    return pl.pallas_call(
        kern,
        grid=(B, S // tm),
        in_specs=[pl.BlockSpec((1, tm, D), lambda b, i: (b, i, 0)),
                  pl.BlockSpec((1, 1, 6, D), lambda b, i: (layer, b, 0, 0)),
                  const((1, D)),
                  const((D, C_END)),
                  pl.BlockSpec((tm, LANES), lambda b, i: (i, 0)),
                  pl.BlockSpec((tm, LANES), lambda b, i: (i, 0)),
                  const((2, 512)),
                  const((4, LANES)),
                  const((512, 512))],
        out_specs=[pl.BlockSpec((1, tm, 512), lambda b, i: (b, i, 0)),
                   pl.BlockSpec((1, N_HEADS // 2, LANES, tm), lambda b, i: (b, 0, 0, i)),
                   grp_rows(LANES), grp_rows(LANES), grp_cols, grp_cols,
                   grp_rows(LANES),
                   pl.BlockSpec((1, KV_GROUPS * GATE_ROWS, tm), lambda b, i: (b, 0, i))],
        out_shape=[jax.ShapeDtypeStruct((B, S, 512), F32),
                   jax.ShapeDtypeStruct((B, N_HEADS // 2, LANES, S), BF16),
                   rows_shape(LANES), rows_shape(LANES), cols_shape, cols_shape,
                   jax.ShapeDtypeStruct((B, KV_GROUPS, S, LANES), F32),
                   jax.ShapeDtypeStruct((B, KV_GROUPS * GATE_ROWS, S), F32)],
        compiler_params=pltpu.CompilerParams(vmem_limit_bytes=VMEM_LIMIT),
    )(x, mod_l, n1, w, cos, sin, gq, gk, bd)


def _compress_kernel(x_ref, pe_ref, w1_ref, w2_ref, kn_ref, cos_ref, sin_ref, ko_ref, vot_ref):
    nc = ko_ref.shape[1]
    taps = CMP_LEN // 2
    first = jnp.zeros((nc, LANES), F32)
    second = jnp.zeros((nc, LANES), F32)
    for t in range(taps):
        x = x_ref[0, 0, pl.ds(t, nc, stride=CMP_STRIDE), :]
        first = first + _dot((x + pe_ref[t:t + 1, :]).astype(BF16), w1_ref[t])
        second = second + _dot((x + pe_ref[taps + t:taps + t + 1, :]).astype(BF16), w1_ref[taps + t])
    pre = first + pltpu.roll(second, nc - 1, axis=0)
    h = (pre * jax.nn.sigmoid(pre)).astype(BF16)
    kkv = _dot(h, w2_ref[...])
    k = kkv[:, 0:LANES]
    kr = kkv[:, LANES:2 * LANES]
    r = lax.rsqrt(jnp.sum(k * k, axis=-1, keepdims=True) * (1.0 / HEAD_DIM) + EPS)
    ko = (k * kn_ref[0:1, :] * cos_ref[...] + kr * kn_ref[1:2, :] * sin_ref[...]) * r
    ko_ref[0] = ko.astype(BF16)
    vot_ref[0] = kkv[:, 2 * LANES:3 * LANES].T.astype(BF16)


def _compress(kvc, layer, pe, w1, w2, kn, cos_c, sin_c):
    B, G, S, _ = kvc.shape
    nc = S // CMP_STRIDE
    const = lambda shape: pl.BlockSpec(shape, lambda b, g: (0,) * len(shape))
    per_layer = lambda a: pl.BlockSpec((None,) + a.shape[1:], lambda b, g: (layer,) + (0,) * (a.ndim - 1))
    return pl.pallas_call(
        _compress_kernel,
        grid=(B, G),
        in_specs=[pl.BlockSpec((1, 1, S, LANES), lambda b, g: (b, g, 0, 0)),
                  per_layer(pe), per_layer(w1), per_layer(w2), per_layer(kn),
                  const(cos_c.shape), const(sin_c.shape)],
        out_specs=[pl.BlockSpec((1, nc, LANES), lambda b, g: (b * G + g, 0, 0)),
                   pl.BlockSpec((1, LANES, nc), lambda b, g: (b * G + g, 0, 0))],
        out_shape=[jax.ShapeDtypeStruct((B * G, nc, LANES), BF16),
                   jax.ShapeDtypeStruct((B * G, LANES, nc), BF16)],
        compiler_params=pltpu.CompilerParams(vmem_limit_bytes=VMEM_LIMIT),
    )(kvc, pe, w1, w2, kn, cos_c, sin_c)


def _nsa_kernel(qt_ref, kc_ref, vct_ref, ov_ref, ks_ref, vst_ref, kw_ref, vwt_ref, gt_ref,
                y_ref, q_sc, acc_sc, o_sc, w_sc, s_sc, p_sc, *, tq, tk, topn):
    i = pl.program_id(2)
    nsub = tq // SUB_Q
    wsub = HEADS_PER_GROUP * SUB_Q
    cols = nsub * wsub
    nc = kc_ref.shape[1]

    def lanes(jq, hh):
        return slice(jq * wsub + hh * SUB_Q, jq * wsub + (hh + 1) * SUB_Q)

    def toks(jq):
        return slice(jq * SUB_Q, (jq + 1) * SUB_Q)

    for jq in range(nsub):
        for hh in range(HEADS_PER_GROUP):
            q_sc[0:HEAD_DIM, lanes(jq, hh)] = qt_ref[
                0, hh // 2, (hh % 2) * HEAD_DIM:(hh % 2 + 1) * HEAD_DIM, toks(jq)]
    q_sc[HEAD_DIM:LANES, :] = jnp.zeros((LANES - HEAD_DIM, cols), BF16)

    def tok(shape, lane0=0):
        lane = lane0 + lax.broadcasted_iota(jnp.int32, shape, 1)
        return i * tq + (lane // wsub) * SUB_Q + (lane & (SUB_Q - 1))

    nwin = WINDOW // SUB_Q
    kk = lax.broadcasted_iota(jnp.int32, (SUB_Q, wsub), 0)
    r = lax.broadcasted_iota(jnp.int32, (SUB_Q, wsub), 1) & (SUB_Q - 1)

    def chunk_start(jq, c):
        return pl.multiple_of(jnp.maximum(i * nsub + jq - nwin + c, 0) * SUB_Q, SUB_Q)

    def window_scores(jq):
        q_sub = q_sc[:, jq * wsub:(jq + 1) * wsub]
        for c in range(nwin + 1):
            sw = _dot(kw_ref[0, 0, pl.ds(chunk_start(jq, c), SUB_Q), :], q_sub)
            if c == nwin:
                sw = jnp.where(kk <= r, sw, NEG)
            else:
                if c == 0:
                    sw = jnp.where(kk > r, sw, NEG)
                sw = sw + jnp.where(i * nsub + jq - nwin + c >= 0, 0.0, NEG)
            w_sc[jq, c * SUB_Q:(c + 1) * SUB_Q, :] = sw

    def window_finish(jq):
        sw = w_sc[jq]
        pw = jnp.exp2(sw - jnp.max(sw, axis=0, keepdims=True)).astype(BF16)
        ow = jnp.zeros((LANES, wsub), F32)
        for c in range(nwin + 1):
            ow = ow + _dot(vwt_ref[0, 0, :, pl.ds(chunk_start(jq, c), SUB_Q)],
                           pw[c * SUB_Q:(c + 1) * SUB_Q])
        o_sc[1, :, jq * wsub:(jq + 1) * wsub] = ow

    s = _dot(kc_ref[0], q_sc[...])
    for jq in range(nsub):
        window_scores(jq)
    n = lax.broadcasted_iota(jnp.int32, (nc, cols), 0)
    mask = n * CMP_STRIDE + (CMP_LEN - 1) <= tok((nc, cols))
    s = jnp.where(mask, s, NEG)
    m = jnp.max(s, axis=0, keepdims=True)
    p = jnp.where(mask, jnp.exp2(s - m), 0.0)
    l = jnp.sum(p, axis=0, keepdims=True)
    p = p * (1.0 / jnp.where(l > 0.0, l, 1.0))
    o_sc[0] = _dot(vct_ref[0], p.astype(BF16))

    sums = []
    for jq in range(nsub):
        ps = p[:, lanes(jq, 0)]
        for hh in range(1, HEADS_PER_GROUP):
            ps = ps + p[:, lanes(jq, hh)]
        sums.append(ps)
    hi, lo = _split_bf16(jnp.concatenate(sums, axis=1))
    ov = ov_ref[...]
    imp = _dot(ov, hi) + _dot(ov, lo)

    for jq in range(nsub - 1):
        window_finish(jq)

    j = lax.broadcasted_iota(jnp.int32, (SEL_SLOTS, tq), 0)
    cur = (i * tq + lax.broadcasted_iota(jnp.int32, (SEL_SLOTS, tq), 1)) // SEL_LEN
    forced = (j == 0) | (j == cur) | (j == cur - 1)
    imp = jnp.where(j <= cur, jnp.where(forced, FORCED_SCORE, imp), -1.0)
    def ranked():
        sub = SUBLANES
        slabs = [imp[a:a + sub] for a in range(0, SEL_SLOTS, sub)]
        ranks = [jnp.zeros((sub, tq), F32) for _ in slabs]
        sub_row = lax.broadcasted_iota(jnp.int32, (sub, tq), 0)
        for jj in range(SEL_SLOTS):
            rowj = jnp.broadcast_to(imp[jj:jj + 1, :], (sub, tq))
            for a, xs in enumerate(slabs):
                if a * sub > jj:
                    inc = jnp.where(rowj >= xs, 1.0, 0.0)
                elif a * sub + sub - 1 <= jj:
                    inc = jnp.where(rowj > xs, 1.0, 0.0)
                else:
                    tie = jnp.where(sub_row > jj - a * sub, 1.0, 0.0)
                    inc = jnp.where(rowj > xs, 1.0, jnp.where(rowj == xs, tie, 0.0))
                ranks[a] = ranks[a] + inc
        rank = jnp.concatenate(ranks, axis=0)
        return jnp.where((rank < float(topn)) & (imp >= 0.0), 0.0, -1.0)

    flag = lax.cond((i + 1) * tq <= topn * SEL_LEN,
                    lambda: jnp.where(imp >= 0.0, 0.0, -1.0), ranked).astype(BF16)

    for jq in range(nsub):
        for hh in range(HEADS_PER_GROUP):
            q_sc[HEAD_DIM:LANES, lanes(jq, hh)] = flag[:, toks(jq)]
    acc_sc[...] = jnp.zeros(acc_sc.shape, F32)

    blocks = [slice(a, a + MXU_COLS) for a in range(0, cols, MXU_COLS)]

    def key_tile(kt):
        return ks_ref[0, 0, pl.ds(pl.multiple_of(kt * tk, tk), tk), :]

    def value_tile(kt):
        return vst_ref[0, 0, :, pl.ds(pl.multiple_of(jnp.maximum(kt, 0) * tk, tk), tk)]

    def score_block(k, slot, cs):
        ss = _dot(k, q_sc[:, cs])
        s_sc[slot, :, cs] = ss
        return jnp.max(ss, axis=0, keepdims=True)

    def softmax_block(slot, cs, m_prev, m_blk, diagonal):
        ss = s_sc[slot, :, cs]
        if diagonal:
            m_blk = jnp.max(ss, axis=0, keepdims=True)
        m_new = jnp.maximum(m_prev, m_blk)
        p_sc[slot, :, cs] = jnp.exp2(ss - m_new).astype(BF16)
        return m_new, jnp.exp2(m_prev - m_new)

    def value_block(vt, slot, cs, a_blk):
        acc_sc[:, cs] = acc_sc[:, cs] * a_blk + _dot(vt, p_sc[slot, :, cs])

    def row(parts):
        return jnp.concatenate(parts, axis=1)

    n_tiles = (i * tq) // tk + 1
    k_first = key_tile(0)
    m_first = row([score_block(k_first, 0, cs) for cs in blocks])
    p_sc[1] = jnp.zeros(p_sc.shape[1:], BF16)
    window_finish(nsub - 1)

    def steady(k, carry):
        m_run, alpha, m_tile = carry

        def arm(slot):
            kx, vt = key_tile(k), value_tile(k - 2)
            m_new, a_new, m_next = [], [], []
            for cs in blocks:
                mn, an = softmax_block(1 - slot, cs, m_run[:, cs], m_tile[:, cs], False)
                m_next.append(score_block(kx, slot, cs))
                value_block(vt, slot, cs, alpha[:, cs])
                m_new.append(mn)
                a_new.append(an)
            return row(m_new), row(a_new), row(m_next)

        return lax.cond(k % 2 == 0, lambda: arm(0), lambda: arm(1))

    m_run, alpha, m_tile = lax.fori_loop(
        1, n_tiles, steady,
        (jnp.full((1, cols), NEG, F32), jnp.ones((1, cols), F32), m_first))

    def drain(slot):
        row0 = pl.multiple_of(i * tq - (n_tiles - 1) * tk, tq)
        key = lax.broadcasted_iota(jnp.int32, (tq, MXU_COLS), 0)
        vt_prev, vt_last = value_tile(n_tiles - 2), value_tile(n_tiles - 1)
        for cs in blocks:
            value_block(vt_prev, 1 - slot, cs, alpha[:, cs])
            seen = key <= tok((tq, MXU_COLS), cs.start) - i * tq
            s_sc[slot, pl.ds(row0, tq), cs] = jnp.where(seen, s_sc[slot, pl.ds(row0, tq), cs], NEG)
            _, a_last = softmax_block(slot, cs, m_run[:, cs], m_tile[:, cs], True)
            value_block(vt_last, slot, cs, a_last)

    @pl.when((n_tiles - 1) % 2 == 0)
    def _():
        drain(0)

    @pl.when((n_tiles - 1) % 2 == 1)
    def _():
        drain(1)

    gt = gt_ref[0]
    tiles = []
    for jq in range(nsub):
        heads = []
        for hh in range(HEADS_PER_GROUP):
            cs = lanes(jq, hh)
            g_c = gt[hh:hh + 1, toks(jq)]
            g_s = gt[HEADS_PER_GROUP + hh:HEADS_PER_GROUP + hh + 1, toks(jq)]
            g_w = gt[2 * HEADS_PER_GROUP + hh:2 * HEADS_PER_GROUP + hh + 1, toks(jq)]
            heads.append(o_sc[0, 0:HEAD_DIM, cs] * g_c
                         + acc_sc[0:HEAD_DIM, cs] * (g_s / acc_sc[HEAD_DIM:HEAD_DIM + 1, cs])
                         + o_sc[1, 0:HEAD_DIM, cs] * (g_w / o_sc[1, HEAD_DIM:HEAD_DIM + 1, cs]))
        tiles.append(jnp.concatenate(heads, axis=0).T)
    y_ref[0] = jnp.concatenate(tiles, axis=0).astype(BF16)


def _nsa(qt, kcmp, vcmpt, ov, ks, vst, kw, vwt, gt, tq, tk, topn):
    B, _, _, S = qt.shape
    nc = kcmp.shape[1]
    cols = HEADS_PER_GROUP * tq
    kern = functools.partial(_nsa_kernel, tq=tq, tk=tk, topn=topn)
    seq_rows = pl.BlockSpec((1, 1, S, LANES), lambda b, g, i: (b, g, 0, 0))
    seq_cols = pl.BlockSpec((1, 1, LANES, S), lambda b, g, i: (b, g, 0, 0))
    return pl.pallas_call(
        kern,
        grid=(B, KV_GROUPS, S // tq),
        in_specs=[pl.BlockSpec((1, HEADS_PER_GROUP // 2, LANES, tq), lambda b, g, i: (b, g, 0, i)),
                  pl.BlockSpec((1, nc, LANES), lambda b, g, i: (b * KV_GROUPS + g, 0, 0)),
                  pl.BlockSpec((1, LANES, nc), lambda b, g, i: (b * KV_GROUPS + g, 0, 0)),
                  pl.BlockSpec((SEL_SLOTS, nc), lambda b, g, i: (0, 0)),
                  seq_rows, seq_cols, seq_rows, seq_cols,
                  pl.BlockSpec((1, GATE_ROWS, tq), lambda b, g, i: (b, g, i))],
        out_specs=pl.BlockSpec((1, tq, HEADS_PER_GROUP * HEAD_DIM), lambda b, g, i: (b, i, g)),
        out_shape=jax.ShapeDtypeStruct((B, S, N_HEADS * HEAD_DIM), BF16),
        scratch_shapes=[pltpu.VMEM((LANES, cols), BF16), pltpu.VMEM((LANES, cols), F32),
                        pltpu.VMEM((2, LANES, cols), F32),
                        pltpu.VMEM((tq // SUB_Q, WINDOW + SUB_Q, HEADS_PER_GROUP * SUB_Q), F32),
                        pltpu.VMEM((2, tk, cols), F32), pltpu.VMEM((2, tk, cols), BF16)],
        compiler_params=pltpu.CompilerParams(vmem_limit_bytes=VMEM_LIMIT),
    )(qt, kcmp, vcmpt, ov, ks, vst, kw, vwt, gt)


def _pool_kernel(u_ref, up_ref, pw_ref, psc_ref, y_ref, ext_sc, *, tm):
    i = pl.program_id(1)
    ext_sc[0:POOL_HALO, :] = jnp.where(i > 0, up_ref[0], 0.0)
    ext_sc[POOL_HALO:POOL_HALO + tm, :] = u_ref[0]
    pos = i * tm + lax.broadcasted_iota(jnp.int32, (tm, LANES), 0)
    for g, w in enumerate(POOL_WINDOWS):
        c0, c1 = g * LANES, (g + 1) * LANES
        cur = ext_sc[POOL_HALO:POOL_HALO + tm, c0:c1]
        acc = cur
        for k in range(1, w):
            acc = acc + ext_sc[POOL_HALO - k:POOL_HALO - k + tm, c0:c1]
        cnt = jnp.minimum(pos + 1, w).astype(F32)
        pooled = acc / cnt - cur
        y = _dot(pooled.astype(BF16), pw_ref[g]) * psc_ref[:, c0:c1]
        y_ref[0, :, c0:c1] = y.astype(BF16)


def _pool(u, pw, psc, tm):
    B, S, width = u.shape
    kern = functools.partial(_pool_kernel, tm=tm)
    per = tm // POOL_HALO
    return pl.pallas_call(
        kern,
        grid=(B, S // tm),
        in_specs=[pl.BlockSpec((1, tm, width), lambda b, i: (b, i, 0)),
                  pl.BlockSpec((1, POOL_HALO, width), lambda b, i: (b, jnp.maximum(i * per - 1, 0), 0)),
                  pl.BlockSpec(pw.shape, lambda b, i: (0, 0, 0)),
                  pl.BlockSpec((1, width), lambda b, i: (0, 0))],
        out_specs=pl.BlockSpec((1, tm, width), lambda b, i: (b, i, 0)),
        out_shape=jax.ShapeDtypeStruct((B, S, width), BF16),
        scratch_shapes=[pltpu.VMEM((tm + POOL_HALO, width), F32)],
        compiler_params=pltpu.CompilerParams(vmem_limit_bytes=VMEM_LIMIT),
    )(u, u, pw, psc)


def _out_ffn_kernel(x_ref, mod_ref, n2_ref, yp_ref, yn_ref, wo_ref, w1_ref, w2_ref, o_ref, *, ff_chunk):
    x = x_ref[0]
    gate1 = mod_ref[0, 0, 2:3, :]
    shift2 = mod_ref[0, 0, 3:4, :]
    scale2 = mod_ref[0, 0, 4:5, :]
    gate2 = mod_ref[0, 0, 5:6, :]
    half = wo_ref.shape[0] // 2
    mix = _dot(yp_ref[0], wo_ref[0:half, :]) + _dot(yn_ref[0], wo_ref[half:2 * half, :])
    x1 = x + gate1 * mix
    ms = jnp.mean(x1 * x1, axis=-1, keepdims=True)
    h2 = ((x1 * lax.rsqrt(ms + EPS) * n2_ref[...]) * (1.0 + scale2) + shift2).astype(BF16)
    d_ff = w1_ref.shape[1]
    ff = jnp.zeros(x.shape, F32)
    for c in range(d_ff // ff_chunk):
        a = jnp.maximum(_dot(h2, w1_ref[:, c * ff_chunk:(c + 1) * ff_chunk]), 0.0)
        ff = ff + _dot((a * a).astype(BF16), w2_ref[c * ff_chunk:(c + 1) * ff_chunk, :])
    o_ref[0] = x1 + gate2 * ff


def _out_ffn(x, mod_l, layer, n2, yp, yn, wo, w1, w2, tm):
    B, S, D = x.shape
    kern = functools.partial(_out_ffn_kernel, ff_chunk=1024)
    tok = lambda width: pl.BlockSpec((1, tm, width), lambda b, i: (b, i, 0))
    const = lambda shape: pl.BlockSpec(shape, lambda b, i: (0,) * len(shape),
                                       pipeline_mode=pl.Buffered(1))
    return pl.pallas_call(
        kern,
        grid=(B, S // tm),
        in_specs=[tok(D),
                  pl.BlockSpec((1, 1, 6, D), lambda b, i: (layer, b, 0, 0)),
                  const((1, D)),
                  tok(512), tok(512),
                  const(wo.shape), const(w1.shape), const(w2.shape)],
        out_specs=tok(D),
        out_shape=jax.ShapeDtypeStruct((B, S, D), F32),
        compiler_params=pltpu.CompilerParams(vmem_limit_bytes=VMEM_LIMIT),
    )(x, mod_l, n2, yp, yn, wo, w1, w2)


def _rot_cols(w):
    shp = w.shape
    w4 = w.reshape(shp[:-1] + (shp[-1] // HEAD_DIM, 2, HEAD_DIM // 2))
    return jnp.stack([-w4[..., 1, :], w4[..., 0, :]], axis=-2).reshape(shp)


def _swap_halves(g):
    half = HEAD_DIM // 2
    return jnp.concatenate([g[..., half:], g[..., :half]], axis=-1)


def _rope_tables(pos, reps):
    half = HEAD_DIM // 2
    inv = ROPE_THETA ** (-jnp.arange(half, dtype=F32) * 2.0 / HEAD_DIM)
    ang = pos.astype(F32)[:, None] * inv[None, :]
    cos = jnp.tile(jnp.cos(ang), (1, 2 * reps))
    sin = jnp.tile(jnp.sin(ang), (1, 2 * reps))
    return cos, sin


def _overlap(n_chunks, n_sel):
    cs0 = np.arange(n_chunks) * CMP_STRIDE
    ss0 = np.arange(n_sel) * SEL_LEN
    ov = np.minimum(cs0[:, None] + CMP_LEN, ss0[None, :] + SEL_LEN) - np.maximum(cs0[:, None], ss0[None, :])
    ov = np.clip(ov, 0, None).astype(np.float32) / CMP_LEN
    out = np.zeros((SEL_SLOTS, n_chunks), np.float32)
    out[:n_sel] = ov.T
    return out


def _gate_cols(w_g):
    idx = np.zeros((LANES,), np.int32)
    used = np.zeros((LANES,), np.float32)
    for g in range(KV_GROUPS):
        for br in range(N_BRANCH):
            for hg in range(HEADS_PER_GROUP):
                c = g * GATE_ROWS + br * HEADS_PER_GROUP + hg
                idx[c] = (g * HEADS_PER_GROUP + hg) * N_BRANCH + br
                used[c] = 1.0
    return w_g[..., idx] * used


def kernel(x, c, w_mod, b_mod, norm1, norm2, w_in, pool_w, pool_scale, q_norm, k_norm, cmp_pe, cmp_w1, cmp_w2, w_out, w_ff1, w_ff2):
    B, S, D = x.shape
    depth = w_mod.shape[0]
    n_chunks = S // CMP_STRIDE
    n_sel = S // SEL_LEN
    assert D == 1024 and S % 2048 == 0 and n_sel <= SEL_SLOTS
    topn = min(SEL_TOPN, n_sel)
    tm_in, tm_out, tm_pool, tq, tk = 256, 256, 512, 256, 512

    mod = _modulation(c, w_mod, b_mod)

    o0, o1, o2 = 512, 1024, 1024 + 768
    w_q = w_in[:, :, o0:o1]
    w_kv = w_in[:, :, o1:o2]
    w_all = jnp.concatenate([w_in[:, :, :o0], w_q, _rot_cols(w_q), w_kv,
                             _rot_cols(w_kv[:, :, 128:384]), _gate_cols(w_in[:, :, o2:])],
                            axis=-1).astype(BF16)
    gq = jnp.stack([jnp.tile(q_norm, (1, N_HEADS)), jnp.tile(_swap_halves(q_norm), (1, N_HEADS))], axis=1)
    gk = jnp.stack([jnp.tile(k_norm[:, 1], (1, 2)), jnp.tile(_swap_halves(k_norm[:, 1]), (1, 2)),
                    jnp.tile(k_norm[:, 2], (1, 2)), jnp.tile(_swap_halves(k_norm[:, 2]), (1, 2))], axis=1)
    pad64 = ((0, 0), (0, HEAD_DIM))
    kn = jnp.stack([jnp.pad(k_norm[:, 0], pad64), jnp.pad(_swap_halves(k_norm[:, 0]), pad64)], axis=1)
    w1_taps = cmp_w1.reshape(depth, 2, CMP_LEN, HEAD_DIM, HEAD_DIM)
    zero_tap = jnp.zeros_like(w1_taps[:, 0])
    w1c = jnp.concatenate([jnp.concatenate([w1_taps[:, 0], zero_tap], axis=-1),
                           jnp.concatenate([zero_tap, w1_taps[:, 1]], axis=-1)], axis=-2).astype(BF16)
    zero_w2 = jnp.zeros_like(cmp_w2[:, 0])
    w2c = jnp.concatenate([
        jnp.concatenate([cmp_w2[:, 0], zero_w2, _rot_cols(cmp_w2[:, 0]), zero_w2, zero_w2, zero_w2], axis=-1),
        jnp.concatenate([zero_w2, zero_w2, zero_w2, zero_w2, cmp_w2[:, 1], zero_w2], axis=-1)],
        axis=-2).astype(BF16)
    pe = jnp.concatenate([cmp_pe[:, 0], cmp_pe[:, 1]], axis=-1)
    pool_wb = pool_w.astype(BF16)
    wo_b, w1_b, w2_b = w_out.astype(BF16), w_ff1.astype(BF16), w_ff2.astype(BF16)

    cos, sin = _rope_tables(jnp.arange(S), 2)
    cos_c, sin_c = _rope_tables(jnp.arange(n_chunks) * CMP_STRIDE + (CMP_LEN - 1), 2)
    bd = jnp.asarray(np.kron(np.eye(N_HEADS, dtype=np.float32),
                             np.full((HEAD_DIM, HEAD_DIM), 1.0 / HEAD_DIM, np.float32)), BF16)
    ov = jnp.asarray(_overlap(n_chunks, n_sel), BF16)

    for l in range(depth):
        u, qt, ks, kw, vst, vwt, kvc, gt = _inproj(
            x, mod, l, norm1[l:l + 1], w_all[l], cos, sin, gq[l], gk[l], bd, tm_in)
        kcmp, vcmpt = _compress(kvc, l, pe, w1c, w2c, kn, cos_c, sin_c)
        y_nsa = _nsa(qt, kcmp, vcmpt, ov, ks, vst, kw, vwt, gt, tq, tk, topn)
        y_pool = _pool(u, pool_wb[l], pool_scale[l:l + 1], tm_pool)
        x = _out_ffn(x, mod, l, norm2[l:l + 1], y_pool, y_nsa, wo_b[l], w1_b[l], w2_b[l], tm_out)
    return x
```

```python
import functools
import math

import numpy as np
import jax
import jax.numpy as jnp
from jax import lax
from jax.experimental import pallas as pl
from jax.experimental.pallas import tpu as pltpu

F32 = jnp.float32
BF16 = jnp.bfloat16

LANES = 128
SUBLANES = 8
HEAD_DIM = 64
N_HEADS = 8
KV_GROUPS = 2
HEADS_PER_GROUP = N_HEADS // KV_GROUPS
N_BRANCH = 3
GATE_ROWS = 16
POOL_WINDOWS = (2, 4, 8, 16)
POOL_HALO = 16
CMP_LEN = 32
CMP_STRIDE = 16
SEL_LEN = 64
SEL_SLOTS = 64
SEL_TOPN = 16
SUB_Q = 128
WINDOW = 512
ROPE_THETA = 10000.0
EPS = 1e-6
NEG = -1e30
FORCED_SCORE = 1e4
LOG2E = math.log2(math.e)
SCALE = 1.0 / math.sqrt(HEAD_DIM)
VMEM_LIMIT = 56 * 1024 * 1024
MXU_COLS = 256


def _dot(a, b):
    return jnp.dot(a, b, preferred_element_type=F32)


def _split_bf16(v):
    hi = v.astype(BF16)
    lo = (v - hi.astype(F32)).astype(BF16)
    return hi, lo


def _mod_kernel(c_ref, w_ref, b_ref, o_ref, *, batch):
    c = c_ref[...]
    rows = c.shape[0]
    ca = c * jax.nn.sigmoid(c)
    hi = ca.astype(BF16).astype(F32)
    row = lax.broadcasted_iota(jnp.int32, ca.shape, 0)
    a = jnp.where(row < batch, hi, ca - hi).astype(BF16)
    w = w_ref[0]
    w_hi, w_lo = _split_bf16(w)
    r1 = _dot(a, w_hi)
    r2 = _dot(a, w_lo)
    o_ref[0] = r1 + pltpu.roll(r1, rows - batch, axis=0) + r2 + b_ref[0]


def _modulation(c, w_mod, b_mod):
    depth, d, n = w_mod.shape
    batch = c.shape[0]
    rows = -(-2 * batch // SUBLANES) * SUBLANES
    c2 = jnp.concatenate([c, c, jnp.zeros((rows - 2 * batch, d), F32)], axis=0)
    tn = 1536
    out = pl.pallas_call(
        functools.partial(_mod_kernel, batch=batch),
        grid=(depth, n // tn),
        in_specs=[pl.BlockSpec((rows, d), lambda l, j: (0, 0)),
                  pl.BlockSpec((1, d, tn), lambda l, j: (l, 0, j)),
                  pl.BlockSpec((1, 1, tn), lambda l, j: (l, 0, j))],
        out_specs=pl.BlockSpec((1, rows, tn), lambda l, j: (l, 0, j)),
        out_shape=jax.ShapeDtypeStruct((depth, rows, n), F32),
        compiler_params=pltpu.CompilerParams(vmem_limit_bytes=VMEM_LIMIT),
    )(c2, w_mod, b_mod.reshape(depth, 1, n))
    return out[:, :batch].reshape(depth, batch, 6, d)


C_U = 0
C_Q = 512
C_KV = 1024
C_END = 1792


def _inproj_kernel(x_ref, mod_ref, n1_ref, w_ref, wg_ref, cos_ref, sin_ref, gq_ref, gk_ref, bd_ref,
                   pw_ref, psc_ref,
                   yp_ref, qt_ref, ks_ref, kw_ref, vst_ref, vwt_ref, kvc_ref, gt_ref, ext_sc, *, tm):
    i = pl.program_id(1)
    x = x_ref[0]
    shift = mod_ref[0, 0, 0:1, :]
    scale = mod_ref[0, 0, 1:2, :]
    ms = jnp.mean(x * x, axis=-1, keepdims=True)
    h = (x * lax.rsqrt(ms + EPS) * n1_ref[...]) * (1.0 + scale) + shift
    hb = h.astype(BF16)

    def proj(a, b):
        return _dot(hb, w_ref[:, a:b])

    def seg_mean(v, bdm):
        return _dot((v * v).astype(BF16), bdm)

    def rope(y, cos_t, sin_t):
        width = y.shape[1]
        first = (lax.broadcasted_iota(jnp.int32, y.shape, 1) & (HEAD_DIM // 2)) == 0
        rot = jnp.where(first, pltpu.roll(y, width - HEAD_DIM // 2, axis=1),
                        pltpu.roll(y, HEAD_DIM // 2, axis=1))
        return y * cos_t + rot * sin_t

    @pl.when(i == 0)
    def _():
        ext_sc[0:POOL_HALO, :] = jnp.zeros((POOL_HALO, C_Q), F32)

    @pl.when(i > 0)
    def _():
        ext_sc[0:POOL_HALO, :] = ext_sc[tm:tm + POOL_HALO, :]

    ext_sc[POOL_HALO:POOL_HALO + tm, :] = proj(C_U, C_Q)
    q = proj(C_Q, C_KV)
    kv = proj(C_KV, C_END)
    gates = _dot(hb, wg_ref[...])
    pos = i * tm + lax.broadcasted_iota(jnp.int32, (tm, LANES), 0)
    for g, w in enumerate(POOL_WINDOWS):
        c0, c1 = g * LANES, (g + 1) * LANES
        cur = ext_sc[POOL_HALO:POOL_HALO + tm, c0:c1]
        acc = cur
        for k in range(1, w):
            acc = acc + ext_sc[POOL_HALO - k:POOL_HALO - k + tm, c0:c1]
        cnt = jnp.minimum(pos + 1, w).astype(F32)
        pooled = acc / cnt - cur
        yp_ref[0, :, c0:c1] = (_dot(pooled.astype(BF16), pw_ref[g]) * psc_ref[:, c0:c1]).astype(BF16)

    cos = cos_ref[...]
    sin = sin_ref[...]
    cos4 = jnp.concatenate([cos] * 4, axis=1)
    sin4 = jnp.concatenate([sin] * 4, axis=1)
    rq = lax.rsqrt(seg_mean(q, bd_ref[...]) + EPS)
    qo = rope(q * gq_ref[...], cos4, sin4) * (rq * (SCALE * LOG2E))
    for hp in range(N_HEADS // 2):
        qt_ref[0, hp] = qo[:, LANES * hp:LANES * (hp + 1)].T.astype(BF16)

    bd2 =bd_ref[0:LANES, 0:LANES]
    k_c, k_s, k_w = kv[:, 0:128], kv[:, 128:256], kv[:, 256:384]
    v_c, v_s, v_w = kv[:, 384:512], kv[:, 512:640], kv[:, 640:768]
    kso = rope(k_s * gk_ref[0:1, :], cos, sin) * lax.rsqrt(seg_mean(k_s, bd2) + EPS)
    kwo = rope(k_w * gk_ref[1:2, :], cos, sin) * lax.rsqrt(seg_mean(k_w, bd2) + EPS)

    lane = lax.broadcasted_iota(jnp.int32, (tm, LANES), 1)
    low = lane < HEAD_DIM
    pos = i * tm + lax.broadcasted_iota(jnp.int32, (tm, LANES), 0)
    blk = pos // SEL_LEN
    blk_flag = jnp.where(lane - HEAD_DIM == blk, -NEG, 0.0)
    ones_col = jnp.where(lane == HEAD_DIM, 1.0, 0.0)
    for g in range(KV_GROUPS):
        def grp(a, g=g):
            return pltpu.roll(a, HEAD_DIM, axis=1) if g else a
        ks_ref[0, g] = jnp.where(low, grp(kso), blk_flag).astype(BF16)
        kw_ref[0, g] = jnp.where(low, grp(kwo), 0.0).astype(BF16)
        vst_ref[0, g] = jnp.where(low, grp(v_s), ones_col).T.astype(BF16)
        vwt_ref[0, g] = jnp.where(low, grp(v_w), ones_col).T.astype(BF16)
        kvc_ref[0, g] = jnp.where(low, grp(k_c), v_c if g else pltpu.roll(v_c, HEAD_DIM, axis=1))
    gt_ref[0] = jax.nn.sigmoid(gates).T[0:KV_GROUPS * GATE_ROWS]


def _inproj(x, mod, layer, n1, w, wg, cos, sin, gq, gk, bd, pw, psc, tm):
    B, S, D = x.shape
    kern = functools.partial(_inproj_kernel, tm=tm)
    grp_rows = lambda width: pl.BlockSpec((1, KV_GROUPS, tm, width), lambda b, i: (b, 0, i, 0))
    grp_cols = pl.BlockSpec((1, KV_GROUPS, LANES, tm), lambda b, i: (b, 0, 0, i))
    rows_shape = lambda width: jax.ShapeDtypeStruct((B, KV_GROUPS, S, width), BF16)
    cols_shape = jax.ShapeDtypeStruct((B, KV_GROUPS, LANES, S), BF16)
    const = lambda shape: pl.BlockSpec(shape, lambda b, i: (0,) * len(shape))
    per_layer = lambda a: pl.BlockSpec((None,) + a.shape[1:], lambda b, i: (layer,) + (0,) * (a.ndim - 1))
    return pl.pallas_call(
        kern,
        grid=(B, S // tm),
        in_specs=[pl.BlockSpec((1, tm, D), lambda b, i: (b, i, 0)),
                  pl.BlockSpec((1, 1, 6, D), lambda b, i: (layer, b, 0, 0)),
                  per_layer(n1), per_layer(w), per_layer(wg),
                  pl.BlockSpec((tm, LANES), lambda b, i: (i, 0)),
                  pl.BlockSpec((tm, LANES), lambda b, i: (i, 0)),
                  per_layer(gq), per_layer(gk), const(bd.shape),
                  per_layer(pw), per_layer(psc)],
        out_specs=[pl.BlockSpec((1, tm, C_Q), lambda b, i: (b, i, 0)),
                   pl.BlockSpec((1, N_HEADS // 2, LANES, tm), lambda b, i: (b, 0, 0, i)),
                   grp_rows(LANES), grp_rows(LANES), grp_cols, grp_cols,
                   grp_rows(LANES),
                   pl.BlockSpec((1, KV_GROUPS * GATE_ROWS, tm), lambda b, i: (b, 0, i))],
        out_shape=[jax.ShapeDtypeStruct((B, S, C_Q), BF16),
                   jax.ShapeDtypeStruct((B, N_HEADS // 2, LANES, S), BF16),
                   rows_shape(LANES), rows_shape(LANES), cols_shape, cols_shape,
                   jax.ShapeDtypeStruct((B, KV_GROUPS, S, LANES), F32),
                   jax.ShapeDtypeStruct((B, KV_GROUPS * GATE_ROWS, S), F32)],
        scratch_shapes=[pltpu.VMEM((tm + POOL_HALO, C_Q), F32)],
        compiler_params=pltpu.CompilerParams(vmem_limit_bytes=VMEM_LIMIT),
    )(x, mod, n1, w, wg, cos, sin, gq, gk, bd, pw, psc)


def _compress_kernel(x_ref, pe_ref, w1_ref, w2_ref, kn_ref, cos_ref, sin_ref, ko_ref, vot_ref):
    nc = ko_ref.shape[1]
    taps = CMP_LEN // 2
    first = jnp.zeros((nc, LANES), F32)
    second = jnp.zeros((nc, LANES), F32)
    for t in range(taps):
        x = x_ref[0, 0, pl.ds(t, nc, stride=CMP_STRIDE), :]
        first = first + _dot((x + pe_ref[t:t + 1, :]).astype(BF16), w1_ref[t])
        second = second + _dot((x + pe_ref[taps + t:taps + t + 1, :]).astype(BF16), w1_ref[taps + t])
    pre = first + pltpu.roll(second, nc - 1, axis=0)
    h = (pre * jax.nn.sigmoid(pre)).astype(BF16)
    kkv = _dot(h, w2_ref[...])
    k = kkv[:, 0:LANES]
    kr = kkv[:, LANES:2 * LANES]
    r = lax.rsqrt(jnp.sum(k * k, axis=-1, keepdims=True) * (1.0 / HEAD_DIM) + EPS)
    ko = (k * kn_ref[0:1, :] * cos_ref[...] + kr * kn_ref[1:2, :] * sin_ref[...]) * r
    ko_ref[0] = ko.astype(BF16)
    vot_ref[0] = kkv[:, 2 * LANES:3 * LANES].T.astype(BF16)


def _compress(kvc, layer, pe, w1, w2, kn, cos_c, sin_c):
    B, G, S, _ = kvc.shape
    nc = S // CMP_STRIDE
    const = lambda shape: pl.BlockSpec(shape, lambda b, g: (0,) * len(shape))
    per_layer = lambda a: pl.BlockSpec((None,) + a.shape[1:], lambda b, g: (layer,) + (0,) * (a.ndim - 1))
    return pl.pallas_call(
        _compress_kernel,
        grid=(B, G),
        in_specs=[pl.BlockSpec((1, 1, S, LANES), lambda b, g: (b, g, 0, 0)),
                  per_layer(pe), per_layer(w1), per_layer(w2), per_layer(kn),
                  const(cos_c.shape), const(sin_c.shape)],
        out_specs=[pl.BlockSpec((1, nc, LANES), lambda b, g: (b * G + g, 0, 0)),
                   pl.BlockSpec((1, LANES, nc), lambda b, g: (b * G + g, 0, 0))],
        out_shape=[jax.ShapeDtypeStruct((B * G, nc, LANES), BF16),
                   jax.ShapeDtypeStruct((B * G, LANES, nc), BF16)],
        compiler_params=pltpu.CompilerParams(vmem_limit_bytes=VMEM_LIMIT),
    )(kvc, pe, w1, w2, kn, cos_c, sin_c)


def _nsa_kernel(qt_ref, kc_ref, vct_ref, ov_ref, ks_ref, vst_ref, kw_ref, vwt_ref, gt_ref,
                y_ref, q_sc, acc_sc, o_sc, w_sc, s_sc, p_sc, *, tq, tk, topn):
    i = pl.program_id(2)
    nsub = tq // SUB_Q
    wsub = HEADS_PER_GROUP * SUB_Q
    cols = nsub * wsub
    nc = kc_ref.shape[1]

    def lanes(jq, hh):
        return slice(jq * wsub + hh * SUB_Q, jq * wsub + (hh + 1) * SUB_Q)

    def toks(jq):
        return slice(jq * SUB_Q, (jq + 1) * SUB_Q)

    for jq in range(nsub):
        for hh in range(HEADS_PER_GROUP):
            q_sc[0:HEAD_DIM, lanes(jq, hh)] = qt_ref[
                0, hh // 2, (hh % 2) * HEAD_DIM:(hh % 2 + 1) * HEAD_DIM, toks(jq)]
    q_sc[HEAD_DIM:LANES, :] = jnp.zeros((LANES - HEAD_DIM, cols), BF16)

    def tok(shape, lane0=0):
        lane = lane0 + lax.broadcasted_iota(jnp.int32, shape, 1)
        return i * tq + (lane // wsub) * SUB_Q + (lane & (SUB_Q - 1))

    nwin = WINDOW // SUB_Q
    kk = lax.broadcasted_iota(jnp.int32, (SUB_Q, wsub), 0)
    r = lax.broadcasted_iota(jnp.int32, (SUB_Q, wsub), 1) & (SUB_Q - 1)

    def chunk_start(jq, c):
        return pl.multiple_of(jnp.maximum(i * nsub + jq - nwin + c, 0) * SUB_Q, SUB_Q)

    def window_scores(jq):
        q_sub = q_sc[:, jq * wsub:(jq + 1) * wsub]
        for c in range(nwin + 1):
            sw = _dot(kw_ref[0, 0, pl.ds(chunk_start(jq, c), SUB_Q), :], q_sub)
            if c == nwin:
                sw = jnp.where(kk <= r, sw, NEG)
            else:
                if c == 0:
                    sw = jnp.where(kk > r, sw, NEG)
                sw = sw + jnp.where(i * nsub + jq - nwin + c >= 0, 0.0, NEG)
            w_sc[jq, c * SUB_Q:(c + 1) * SUB_Q, :] = sw

    def window_finish(jq):
        sw = w_sc[jq]
        pw = jnp.exp2(sw - jnp.max(sw, axis=0, keepdims=True)).astype(BF16)
        ow = jnp.zeros((LANES, wsub), F32)
        for c in range(nwin + 1):
            ow = ow + _dot(vwt_ref[0, 0, :, pl.ds(chunk_start(jq, c), SUB_Q)],
                           pw[c * SUB_Q:(c + 1) * SUB_Q])
        o_sc[1, :, jq * wsub:(jq + 1) * wsub] = ow

    s = _dot(kc_ref[0], q_sc[...])
    for jq in range(nsub):
        window_scores(jq)
    n = lax.broadcasted_iota(jnp.int32, (nc, cols), 0)
    mask = n * CMP_STRIDE + (CMP_LEN - 1) <= tok((nc, cols))
    s = jnp.where(mask, s, NEG)
    m = jnp.max(s, axis=0, keepdims=True)
    p = jnp.where(mask, jnp.exp2(s - m), 0.0)
    l = jnp.sum(p, axis=0, keepdims=True)
    p = p * (1.0 / jnp.where(l > 0.0, l, 1.0))
    o_sc[0] = _dot(vct_ref[0], p.astype(BF16))

    sums = []
    for jq in range(nsub):
        ps = p[:, lanes(jq, 0)]
        for hh in range(1, HEADS_PER_GROUP):
            ps = ps + p[:, lanes(jq, hh)]
        sums.append(ps)
    hi, lo = _split_bf16(jnp.concatenate(sums, axis=1))
    ov = ov_ref[...]
    imp = _dot(ov, hi) + _dot(ov, lo)

    for jq in range(nsub - 1):
        window_finish(jq)

    j = lax.broadcasted_iota(jnp.int32, (SEL_SLOTS, tq), 0)
    cur = (i * tq + lax.broadcasted_iota(jnp.int32, (SEL_SLOTS, tq), 1)) // SEL_LEN
    forced = (j == 0) | (j == cur) | (j == cur - 1)
    imp = jnp.where(j <= cur, jnp.where(forced, FORCED_SCORE, imp), -1.0)
    def ranked(n_rows):
        def run():
            sub = SUBLANES
            slabs = [imp[a:a + sub] for a in range(0, n_rows, sub)]
            ranks = [jnp.zeros((sub, tq), F32) for _ in slabs]
            sub_row = lax.broadcasted_iota(jnp.int32, (sub, tq), 0)
            for jj in range(n_rows):
                rowj = jnp.broadcast_to(imp[jj:jj + 1, :], (sub, tq))
                for a, xs in enumerate(slabs):
                    if a * sub > jj:
                        inc = jnp.where(rowj >= xs, 1.0, 0.0)
                    elif a * sub + sub - 1 <= jj:
                        inc = jnp.where(rowj > xs, 1.0, 0.0)
                    else:
                        tie = jnp.where(sub_row > jj - a * sub, 1.0, 0.0)
                        inc = jnp.where(rowj > xs, 1.0, jnp.where(rowj == xs, tie, 0.0))
                    ranks[a] = ranks[a] + inc
            rank = jnp.concatenate(ranks, axis=0)
            kept = jnp.where((rank < float(topn)) & (imp[0:n_rows] >= 0.0), 0.0, -1.0)
            if n_rows == SEL_SLOTS:
                return kept
            return jnp.concatenate([kept, jnp.full((SEL_SLOTS - n_rows, tq), -1.0, F32)], axis=0)
        return run

    row_options = tuple(r for r in (SEL_SLOTS // 2, 3 * SEL_SLOTS // 4) if r > topn) + (SEL_SLOTS,)
    visible = (i + 1) * tq // SEL_LEN
    branch = sum((visible > r).astype(jnp.int32) for r in (topn,) + row_options[:-1])
    flag = lax.switch(branch, [lambda: jnp.where(imp >= 0.0, 0.0, -1.0)]
                      + [ranked(r) for r in row_options]).astype(BF16)

    for jq in range(nsub):
        for hh in range(HEADS_PER_GROUP):
            q_sc[HEAD_DIM:LANES, lanes(jq, hh)] = flag[:, toks(jq)]
    acc_sc[...] = jnp.zeros(acc_sc.shape, F32)

    blocks = [slice(a, a + MXU_COLS) for a in range(0, cols, MXU_COLS)]

    def key_tile(kt):
        return ks_ref[0, 0, pl.ds(pl.multiple_of(kt * tk, tk), tk), :]

    def value_tile(kt):
        return vst_ref[0, 0, :, pl.ds(pl.multiple_of(jnp.maximum(kt, 0) * tk, tk), tk)]

    def score_block(k, slot, cs):
        ss = _dot(k, q_sc[:, cs])
        s_sc[slot, :, cs] = ss
        return jnp.max(ss, axis=0, keepdims=True)

    def softmax_block(slot, cs, m_prev, m_blk, diagonal):
        ss = s_sc[slot, :, cs]
        if diagonal:
            m_blk = jnp.max(ss, axis=0, keepdims=True)
        m_new = jnp.maximum(m_prev, m_blk)
        p_sc[slot, :, cs] = jnp.exp2(ss - m_new).astype(BF16)
        return m_new, jnp.exp2(m_prev - m_new)

    def value_block(vt, slot, cs, a_blk):
        acc_sc[:, cs] = acc_sc[:, cs] * a_blk + _dot(vt, p_sc[slot, :, cs])

    def row(parts):
        return jnp.concatenate(parts, axis=1)

    n_tiles = (i * tq) // tk + 1
    k_first = key_tile(0)
    m_first = row([score_block(k_first, 0, cs) for cs in blocks])
    p_sc[1] = jnp.zeros(p_sc.shape[1:], BF16)
    window_finish(nsub - 1)

    def steady(k, carry):
        m_run, alpha, m_tile = carry

        def arm(slot):
            kx, vt = key_tile(k), value_tile(k - 2)
            m_new, a_new, m_next = [], [], []
            for cs in blocks:
                mn, an = softmax_block(1 - slot, cs, m_run[:, cs], m_tile[:, cs], False)
                m_next.append(score_block(kx, slot, cs))
                value_block(vt, slot, cs, alpha[:, cs])
                m_new.append(mn)
                a_new.append(an)
            return row(m_new), row(a_new), row(m_next)

        return lax.cond(k % 2 == 0, lambda: arm(0), lambda: arm(1))

    m_run, alpha, m_tile = lax.fori_loop(
        1, n_tiles, steady,
        (jnp.full((1, cols), NEG, F32), jnp.ones((1, cols), F32), m_first))

    def drain(slot):
        row0 = pl.multiple_of(i * tq - (n_tiles - 1) * tk, tq)
        key = lax.broadcasted_iota(jnp.int32, (tq, MXU_COLS), 0)
        vt_prev, vt_last = value_tile(n_tiles - 2), value_tile(n_tiles - 1)
        for cs in blocks:
            value_block(vt_prev, 1 - slot, cs, alpha[:, cs])
            seen = key <= tok((tq, MXU_COLS), cs.start) - i * tq
            s_sc[slot, pl.ds(row0, tq), cs] = jnp.where(seen, s_sc[slot, pl.ds(row0, tq), cs], NEG)
            _, a_last = softmax_block(slot, cs, m_run[:, cs], m_tile[:, cs], True)
            value_block(vt_last, slot, cs, a_last)

    @pl.when((n_tiles - 1) % 2 == 0)
    def _():
        drain(0)

    @pl.when((n_tiles - 1) % 2 == 1)
    def _():
        drain(1)

    gt = gt_ref[0]
    tiles = []
    for jq in range(nsub):
        heads = []
        for hh in range(HEADS_PER_GROUP):
            cs = lanes(jq, hh)
            g_c = gt[hh:hh + 1, toks(jq)]
            g_s = gt[HEADS_PER_GROUP + hh:HEADS_PER_GROUP + hh + 1, toks(jq)]
            g_w = gt[2 * HEADS_PER_GROUP + hh:2 * HEADS_PER_GROUP + hh + 1, toks(jq)]
            heads.append(o_sc[0, 0:HEAD_DIM, cs] * g_c
                         + acc_sc[0:HEAD_DIM, cs] * (g_s / acc_sc[HEAD_DIM:HEAD_DIM + 1, cs])
                         + o_sc[1, 0:HEAD_DIM, cs] * (g_w / o_sc[1, HEAD_DIM:HEAD_DIM + 1, cs]))
        tiles.append(jnp.concatenate(heads, axis=0).T)
    y_ref[0] = jnp.concatenate(tiles, axis=0).astype(BF16)


def _nsa(qt, kcmp, vcmpt, ov, ks, vst, kw, vwt, gt, tq, tk, topn):
    B, _, _, S = qt.shape
    nc = kcmp.shape[1]
    cols = HEADS_PER_GROUP * tq
    kern = functools.partial(_nsa_kernel, tq=tq, tk=tk, topn=topn)
    seq_rows = pl.BlockSpec((1, 1, S, LANES), lambda b, g, i: (b, g, 0, 0))
    seq_cols = pl.BlockSpec((1, 1, LANES, S), lambda b, g, i: (b, g, 0, 0))
    return pl.pallas_call(
        kern,
        grid=(B, KV_GROUPS, S // tq),
        in_specs=[pl.BlockSpec((1, HEADS_PER_GROUP // 2, LANES, tq), lambda b, g, i: (b, g, 0, i)),
                  pl.BlockSpec((1, nc, LANES), lambda b, g, i: (b * KV_GROUPS + g, 0, 0)),
                  pl.BlockSpec((1, LANES, nc), lambda b, g, i: (b * KV_GROUPS + g, 0, 0)),
                  pl.BlockSpec((SEL_SLOTS, nc), lambda b, g, i: (0, 0)),
                  seq_rows, seq_cols, seq_rows, seq_cols,
                  pl.BlockSpec((1, GATE_ROWS, tq), lambda b, g, i: (b, g, i))],
        out_specs=pl.BlockSpec((1, tq, HEADS_PER_GROUP * HEAD_DIM), lambda b, g, i: (b, i, g)),
        out_shape=jax.ShapeDtypeStruct((B, S, N_HEADS * HEAD_DIM), BF16),
        scratch_shapes=[pltpu.VMEM((LANES, cols), BF16), pltpu.VMEM((LANES, cols), F32),
                        pltpu.VMEM((2, LANES, cols), F32),
                        pltpu.VMEM((tq // SUB_Q, WINDOW + SUB_Q, HEADS_PER_GROUP * SUB_Q), F32),
                        pltpu.VMEM((2, tk, cols), F32), pltpu.VMEM((2, tk, cols), BF16)],
        compiler_params=pltpu.CompilerParams(vmem_limit_bytes=VMEM_LIMIT),
    )(qt, kcmp, vcmpt, ov, ks, vst, kw, vwt, gt)


def _out_ffn_kernel(x_ref, mod_ref, n2_ref, yp_ref, yn_ref, wo_ref, w1_ref, w2_ref, o_ref, *, ff_chunk):
    x = x_ref[0]
    gate1 = mod_ref[0, 0, 2:3, :]
    shift2 = mod_ref[0, 0, 3:4, :]
    scale2 = mod_ref[0, 0, 4:5, :]
    gate2 = mod_ref[0, 0, 5:6, :]
    half = wo_ref.shape[0] // 2
    mix = _dot(yp_ref[0], wo_ref[0:half, :]) + _dot(yn_ref[0], wo_ref[half:2 * half, :])
    x1 = x + gate1 * mix
    ms = jnp.mean(x1 * x1, axis=-1, keepdims=True)
    h2 = ((x1 * lax.rsqrt(ms + EPS) * n2_ref[...]) * (1.0 + scale2) + shift2).astype(BF16)
    d_ff = w1_ref.shape[1]
    ff = jnp.zeros(x.shape, F32)
    for c in range(d_ff // ff_chunk):
        a = jnp.maximum(_dot(h2, w1_ref[:, c * ff_chunk:(c + 1) * ff_chunk]), 0.0)
        ff = ff + _dot((a * a).astype(BF16), w2_ref[c * ff_chunk:(c + 1) * ff_chunk, :])
    o_ref[0] = x1 + gate2 * ff


def _out_ffn(x, mod_l, layer, n2, yp, yn, wo, w1, w2, tm):
    B, S, D = x.shape
    kern = functools.partial(_out_ffn_kernel, ff_chunk=1024)
    tok = lambda width: pl.BlockSpec((1, tm, width), lambda b, i: (b, i, 0))
    const = lambda shape: pl.BlockSpec(shape, lambda b, i: (0,) * len(shape),
                                       pipeline_mode=pl.Buffered(1))
    return pl.pallas_call(
        kern,
        grid=(B, S // tm),
        in_specs=[tok(D),
                  pl.BlockSpec((1, 1, 6, D), lambda b, i: (layer, b, 0, 0)),
                  const((1, D)),
                  tok(512), tok(512),
                  const(wo.shape), const(w1.shape), const(w2.shape)],
        out_specs=tok(D),
        out_shape=jax.ShapeDtypeStruct((B, S, D), F32),
        compiler_params=pltpu.CompilerParams(vmem_limit_bytes=VMEM_LIMIT),
    )(x, mod_l, n2, yp, yn, wo, w1, w2)


def _rot_cols(w):
    shp = w.shape
    w4 = w.reshape(shp[:-1] + (shp[-1] // HEAD_DIM, 2, HEAD_DIM // 2))
    return jnp.stack([-w4[..., 1, :], w4[..., 0, :]], axis=-2).reshape(shp)


def _swap_halves(g):
    half = HEAD_DIM // 2
    return jnp.concatenate([g[..., half:], g[..., :half]], axis=-1)


def _rope_tables(pos, reps, signed):
    half = HEAD_DIM // 2
    inv = ROPE_THETA ** (-jnp.arange(half, dtype=F32) * 2.0 / HEAD_DIM)
    ang = pos.astype(F32)[:, None] * inv[None, :]
    cos = jnp.tile(jnp.cos(ang), (1, 2 * reps))
    sin = jnp.sin(ang)
    sin = jnp.tile(jnp.concatenate([-sin if signed else sin, sin], axis=1), (1, reps))
    return cos, sin


def _overlap(n_chunks, n_sel):
    cs0 = np.arange(n_chunks) * CMP_STRIDE
    ss0 = np.arange(n_sel) * SEL_LEN
    ov = np.minimum(cs0[:, None] + CMP_LEN, ss0[None, :] + SEL_LEN) - np.maximum(cs0[:, None], ss0[None, :])
    ov = np.clip(ov, 0, None).astype(np.float32) / CMP_LEN
    out = np.zeros((SEL_SLOTS, n_chunks), np.float32)
    out[:n_sel] = ov.T
    return out


def _gate_cols(w_g):
    idx = np.zeros((LANES,), np.int32)
    used = np.zeros((LANES,), np.float32)
    for g in range(KV_GROUPS):
        for br in range(N_BRANCH):
            for hg in range(HEADS_PER_GROUP):
                c = g * GATE_ROWS + br * HEADS_PER_GROUP + hg
                idx[c] = (g * HEADS_PER_GROUP + hg) * N_BRANCH + br
                used[c] = 1.0
    return w_g[..., idx] * used


def kernel(x, c, w_mod, b_mod, norm1, norm2, w_in, pool_w, pool_scale, q_norm, k_norm, cmp_pe, cmp_w1, cmp_w2, w_out, w_ff1, w_ff2):
    B, S, D = x.shape
    depth = w_mod.shape[0]
    n_chunks = S // CMP_STRIDE
    n_sel = S // SEL_LEN
    assert D == 1024 and S % 2048 == 0 and n_sel <= SEL_SLOTS
    topn = min(SEL_TOPN, n_sel)
    tm_in, tm_out, tq, tk = 512, 512, 512, 512

    mod = _modulation(c, w_mod, b_mod)

    w_b = w_in.astype(BF16)
    w_g = _gate_cols(w_in[:, :, C_END:]).astype(BF16)
    gq = jnp.tile(q_norm, (1, N_HEADS))[:, None, :]
    gk = jnp.stack([jnp.tile(k_norm[:, 1], (1, 2)), jnp.tile(k_norm[:, 2], (1, 2))], axis=1)
    pad64 = ((0, 0), (0, HEAD_DIM))
    kn = jnp.stack([jnp.pad(k_norm[:, 0], pad64), jnp.pad(_swap_halves(k_norm[:, 0]), pad64)], axis=1)
    w1_taps = cmp_w1.reshape(depth, 2, CMP_LEN, HEAD_DIM, HEAD_DIM)
    zero_tap = jnp.zeros_like(w1_taps[:, 0])
    w1c = jnp.concatenate([jnp.concatenate([w1_taps[:, 0], zero_tap], axis=-1),
                           jnp.concatenate([zero_tap, w1_taps[:, 1]], axis=-1)], axis=-2).astype(BF16)
    zero_w2 = jnp.zeros_like(cmp_w2[:, 0])
    w2c = jnp.concatenate([
        jnp.concatenate([cmp_w2[:, 0], zero_w2, _rot_cols(cmp_w2[:, 0]), zero_w2, zero_w2, zero_w2], axis=-1),
        jnp.concatenate([zero_w2, zero_w2, zero_w2, zero_w2, cmp_w2[:, 1], zero_w2], axis=-1)],
        axis=-2).astype(BF16)
    pe = jnp.concatenate([cmp_pe[:, 0], cmp_pe[:, 1]], axis=-1)
    pool_wb = pool_w.astype(BF16)
    wo_b, w1_b, w2_b = w_out.astype(BF16), w_ff1.astype(BF16), w_ff2.astype(BF16)

    cos, sin = _rope_tables(jnp.arange(S), 2, signed=True)
    cos_c, sin_c = _rope_tables(jnp.arange(n_chunks) * CMP_STRIDE + (CMP_LEN - 1), 2, signed=False)
    bd = jnp.asarray(np.kron(np.eye(N_HEADS, dtype=np.float32),
                             np.full((HEAD_DIM, HEAD_DIM), 1.0 / HEAD_DIM, np.float32)), BF16)
    ov = jnp.asarray(_overlap(n_chunks, n_sel), BF16)

    for l in range(depth):
        y_pool, qt, ks, kw, vst, vwt, kvc, gt = _inproj(
            x, mod, l, norm1[:, None, :], w_b, w_g, cos, sin, gq, gk, bd,
            pool_wb, pool_scale[:, None, :], tm_in)
        kcmp, vcmpt = _compress(kvc, l, pe, w1c, w2c, kn, cos_c, sin_c)
        y_nsa = _nsa(qt, kcmp, vcmpt, ov, ks, vst, kw, vwt, gt, tq, tk, topn)
        x = _out_ffn(x, mod, l, norm2[l:l + 1], y_pool, y_nsa, wo_b[l], w1_b[l], w2_b[l], tm_out)
    return x
```

```python
import functools
import math

import numpy as np
import jax
import jax.numpy as jnp
from jax import lax
from jax.experimental import pallas as pl
from jax.experimental.pallas import tpu as pltpu

F32 = jnp.float32
BF16 = jnp.bfloat16

LANES = 128
SUBLANES = 8
HEAD_DIM = 64
N_HEADS = 8
KV_GROUPS = 2
HEADS_PER_GROUP = N_HEADS // KV_GROUPS
N_BRANCH = 3
GATE_ROWS = 16
POOL_WINDOWS = (2, 4, 8, 16)
POOL_HALO = 16
CMP_LEN = 32
CMP_STRIDE = 16
SEL_LEN = 64
SEL_SLOTS = 64
SEL_TOPN = 16
SUB_Q = 128
WINDOW = 512
ROPE_THETA = 10000.0
EPS = 1e-6
NEG = -1e30
FORCED_SCORE = 1e4
LOG2E = math.log2(math.e)
SCALE = 1.0 / math.sqrt(HEAD_DIM)
VMEM_LIMIT = 56 * 1024 * 1024
MXU_COLS = 256


def _dot(a, b):
    return jnp.dot(a, b, preferred_element_type=F32)


def _split_bf16(v):
    hi = v.astype(BF16)
    lo = (v - hi.astype(F32)).astype(BF16)
    return hi, lo


def _mod_kernel(c_ref, w_ref, b_ref, o_ref, *, batch):
    c = c_ref[...]
    rows = c.shape[0]
    ca = c * jax.nn.sigmoid(c)
    hi = ca.astype(BF16).astype(F32)
    row = lax.broadcasted_iota(jnp.int32, ca.shape, 0)
    a = jnp.where(row < batch, hi, ca - hi).astype(BF16)
    w = w_ref[0]
    w_hi, w_lo = _split_bf16(w)
    r1 = _dot(a, w_hi)
    r2 = _dot(a, w_lo)
    o_ref[0] = r1 + pltpu.roll(r1, rows - batch, axis=0) + r2 + b_ref[0]


def _modulation(c, w_mod, b_mod):
    depth, d, n = w_mod.shape
    batch = c.shape[0]
    rows = -(-2 * batch // SUBLANES) * SUBLANES
    c2 = jnp.concatenate([c, c, jnp.zeros((rows - 2 * batch, d), F32)], axis=0)
    tn = 1536
    out = pl.pallas_call(
        functools.partial(_mod_kernel, batch=batch),
        grid=(depth, n // tn),
        in_specs=[pl.BlockSpec((rows, d), lambda l, j: (0, 0)),
                  pl.BlockSpec((1, d, tn), lambda l, j: (l, 0, j)),
                  pl.BlockSpec((1, 1, tn), lambda l, j: (l, 0, j))],
        out_specs=pl.BlockSpec((1, rows, tn), lambda l, j: (l, 0, j)),
        out_shape=jax.ShapeDtypeStruct((depth, rows, n), F32),
        compiler_params=pltpu.CompilerParams(vmem_limit_bytes=VMEM_LIMIT),
    )(c2, w_mod, b_mod.reshape(depth, 1, n))
    return out[:, :batch].reshape(depth, batch, 6, d)


C_U = 0
C_Q = 512
C_KV = 1024
C_END = 1792


def _inproj_kernel(x_ref, mod_ref, n1_ref, w_ref, wg_ref, cos_ref, sin_ref, gq_ref, gk_ref, bd_ref,
                   pw_ref, psc_ref,
                   yp_ref, qt_ref, ks_ref, kw_ref, vst_ref, vwt_ref, kvc_ref, gt_ref, ext_sc, *, tm):
    i = pl.program_id(1)
    x = x_ref[0]
    shift = mod_ref[0, 0, 0:1, :]
    scale = mod_ref[0, 0, 1:2, :]
    ms = jnp.mean(x * x, axis=-1, keepdims=True)
    h = (x * lax.rsqrt(ms + EPS) * n1_ref[...]) * (1.0 + scale) + shift
    hb = h.astype(BF16)

    def proj(a, b):
        return _dot(hb, w_ref[:, a:b])

    def seg_mean(v, bdm):
        return _dot((v * v).astype(BF16), bdm)

    def rope(y, cos_t, sin_t):
        width = y.shape[1]
        first = (lax.broadcasted_iota(jnp.int32, y.shape, 1) & (HEAD_DIM // 2)) == 0
        rot = jnp.where(first, pltpu.roll(y, width - HEAD_DIM // 2, axis=1),
                        pltpu.roll(y, HEAD_DIM // 2, axis=1))
        return y * cos_t + rot * sin_t

    @pl.when(i == 0)
    def _():
        ext_sc[0:POOL_HALO, :] = jnp.zeros((POOL_HALO, C_Q), F32)

    @pl.when(i > 0)
    def _():
        ext_sc[0:POOL_HALO, :] = ext_sc[tm:tm + POOL_HALO, :]

    ext_sc[POOL_HALO:POOL_HALO + tm, :] = proj(C_U, C_Q)
    q = proj(C_Q, C_KV)
    kv = proj(C_KV, C_END)
    gates = _dot(hb, wg_ref[...])
    pos = i * tm + lax.broadcasted_iota(jnp.int32, (tm, LANES), 0)
    for g, w in enumerate(POOL_WINDOWS):
        c0, c1 = g * LANES, (g + 1) * LANES
        cur = ext_sc[POOL_HALO:POOL_HALO + tm, c0:c1]
        acc = cur
        for k in range(1, w):
            acc = acc + ext_sc[POOL_HALO - k:POOL_HALO - k + tm, c0:c1]
        cnt = jnp.minimum(pos + 1, w).astype(F32)
        pooled = acc / cnt - cur
        yp_ref[0, :, c0:c1] = (_dot(pooled.astype(BF16), pw_ref[g]) * psc_ref[:, c0:c1]).astype(BF16)

    cos = cos_ref[...]
    sin = sin_ref[...]
    cos4 = jnp.concatenate([cos] * 4, axis=1)
    sin4 = jnp.concatenate([sin] * 4, axis=1)
    rq = lax.rsqrt(seg_mean(q, bd_ref[...]) + EPS)
    qo = rope(q * gq_ref[...], cos4, sin4) * (rq * (SCALE * LOG2E))
    for hp in range(N_HEADS // 2):
        qt_ref[0, hp] = qo[:, LANES * hp:LANES * (hp + 1)].T.astype(BF16)

    bd2 =bd_ref[0:LANES, 0:LANES]
    k_c, k_s, k_w = kv[:, 0:128], kv[:, 128:256], kv[:, 256:384]
    v_c, v_s, v_w = kv[:, 384:512], kv[:, 512:640], kv[:, 640:768]
    kso = rope(k_s * gk_ref[0:1, :], cos, sin) * lax.rsqrt(seg_mean(k_s, bd2) + EPS)
    kwo = rope(k_w * gk_ref[1:2, :], cos, sin) * lax.rsqrt(seg_mean(k_w, bd2) + EPS)

    lane = lax.broadcasted_iota(jnp.int32, (tm, LANES), 1)
    low = lane < HEAD_DIM
    pos = i * tm + lax.broadcasted_iota(jnp.int32, (tm, LANES), 0)
    blk = pos // SEL_LEN
    blk_flag = jnp.where(lane - HEAD_DIM == blk, -NEG, 0.0)
    ones_col = jnp.where(lane == HEAD_DIM, 1.0, 0.0)
    for g in range(KV_GROUPS):
        def grp(a, g=g):
            return pltpu.roll(a, HEAD_DIM, axis=1) if g else a
        ks_ref[0, g] = jnp.where(low, grp(kso), blk_flag).astype(BF16)
        kw_ref[0, g] = jnp.where(low, grp(kwo), 0.0).astype(BF16)
        vst_ref[0, g] = jnp.where(low, grp(v_s), ones_col).T.astype(BF16)
        vwt_ref[0, g] = jnp.where(low, grp(v_w), ones_col).T.astype(BF16)
        kvc_ref[0, g] = jnp.where(low, grp(k_c), v_c if g else pltpu.roll(v_c, HEAD_DIM, axis=1))
    gt_ref[0] = jax.nn.sigmoid(gates).T[0:KV_GROUPS * GATE_ROWS]


def _inproj(x, mod, layer, n1, w, wg, cos, sin, gq, gk, bd, pw, psc, tm):
    B, S, D = x.shape
    kern = functools.partial(_inproj_kernel, tm=tm)
    grp_rows = lambda width: pl.BlockSpec((1, KV_GROUPS, tm, width), lambda b, i: (b, 0, i, 0))
    grp_cols = pl.BlockSpec((1, KV_GROUPS, LANES, tm), lambda b, i: (b, 0, 0, i))
    rows_shape = lambda width: jax.ShapeDtypeStruct((B, KV_GROUPS, S, width), BF16)
    cols_shape = jax.ShapeDtypeStruct((B, KV_GROUPS, LANES, S), BF16)
    const = lambda shape: pl.BlockSpec(shape, lambda b, i: (0,) * len(shape))
    per_layer = lambda a: pl.BlockSpec((None,) + a.shape[1:], lambda b, i: (layer,) + (0,) * (a.ndim - 1))
    return pl.pallas_call(
        kern,
        grid=(B, S // tm),
        in_specs=[pl.BlockSpec((1, tm, D), lambda b, i: (b, i, 0)),
                  pl.BlockSpec((1, 1, 6, D), lambda b, i: (layer, b, 0, 0)),
                  per_layer(n1), per_layer(w), per_layer(wg),
                  pl.BlockSpec((tm, LANES), lambda b, i: (i, 0)),
                  pl.BlockSpec((tm, LANES), lambda b, i: (i, 0)),
                  per_layer(gq), per_layer(gk), const(bd.shape),
                  per_layer(pw), per_layer(psc)],
        out_specs=[pl.BlockSpec((1, tm, C_Q), lambda b, i: (b, i, 0)),
                   pl.BlockSpec((1, N_HEADS // 2, LANES, tm), lambda b, i: (b, 0, 0, i)),
                   grp_rows(LANES), grp_rows(LANES), grp_cols, grp_cols,
                   grp_rows(LANES),
                   pl.BlockSpec((1, KV_GROUPS * GATE_ROWS, tm), lambda b, i: (b, 0, i))],
        out_shape=[jax.ShapeDtypeStruct((B, S, C_Q), BF16),
                   jax.ShapeDtypeStruct((B, N_HEADS // 2, LANES, S), BF16),
                   rows_shape(LANES), rows_shape(LANES), cols_shape, cols_shape,
                   jax.ShapeDtypeStruct((B, KV_GROUPS, S, LANES), F32),
                   jax.ShapeDtypeStruct((B, KV_GROUPS * GATE_ROWS, S), F32)],
        scratch_shapes=[pltpu.VMEM((tm + POOL_HALO, C_Q), F32)],
        compiler_params=pltpu.CompilerParams(vmem_limit_bytes=VMEM_LIMIT),
    )(x, mod, n1, w, wg, cos, sin, gq, gk, bd, pw, psc)


def _compress_kernel(x_ref, pe_ref, w1_ref, w2_ref, kn_ref, cos_ref, sin_ref, ko_ref, vot_ref):
    nc = ko_ref.shape[1]
    taps = CMP_LEN // 2
    first = jnp.zeros((nc, LANES), F32)
    second = jnp.zeros((nc, LANES), F32)
    for t in range(taps):
        x = x_ref[0, 0, pl.ds(t, nc, stride=CMP_STRIDE), :]
        first = first + _dot((x + pe_ref[t:t + 1, :]).astype(BF16), w1_ref[t])
        second = second + _dot((x + pe_ref[taps + t:taps + t + 1, :]).astype(BF16), w1_ref[taps + t])
    pre = first + pltpu.roll(second, nc - 1, axis=0)
    h = (pre * jax.nn.sigmoid(pre)).astype(BF16)
    kkv = _dot(h, w2_ref[...])
    k = kkv[:, 0:LANES]
    kr = kkv[:, LANES:2 * LANES]
    r = lax.rsqrt(jnp.sum(k * k, axis=-1, keepdims=True) * (1.0 / HEAD_DIM) + EPS)
    ko = (k * kn_ref[0:1, :] * cos_ref[...] + kr * kn_ref[1:2, :] * sin_ref[...]) * r
    ko_ref[0] = ko.astype(BF16)
    vot_ref[0] = kkv[:, 2 * LANES:3 * LANES].T.astype(BF16)


def _compress(kvc, layer, pe, w1, w2, kn, cos_c, sin_c):
    B, G, S, _ = kvc.shape
    nc = S // CMP_STRIDE
    const = lambda shape: pl.BlockSpec(shape, lambda b, g: (0,) * len(shape))
    per_layer = lambda a: pl.BlockSpec((None,) + a.shape[1:], lambda b, g: (layer,) + (0,) * (a.ndim - 1))
    return pl.pallas_call(
        _compress_kernel,
        grid=(B, G),
        in_specs=[pl.BlockSpec((1, 1, S, LANES), lambda b, g: (b, g, 0, 0)),
                  per_layer(pe), per_layer(w1), per_layer(w2), per_layer(kn),
                  const(cos_c.shape), const(sin_c.shape)],
        out_specs=[pl.BlockSpec((1, nc, LANES), lambda b, g: (b * G + g, 0, 0)),
                   pl.BlockSpec((1, LANES, nc), lambda b, g: (b * G + g, 0, 0))],
        out_shape=[jax.ShapeDtypeStruct((B * G, nc, LANES), BF16),
                   jax.ShapeDtypeStruct((B * G, LANES, nc), BF16)],
        compiler_params=pltpu.CompilerParams(vmem_limit_bytes=VMEM_LIMIT),
    )(kvc, pe, w1, w2, kn, cos_c, sin_c)


def _nsa_kernel(qt_ref, kc_ref, vct_ref, ov_ref, ks_ref, vst_ref, kw_ref, vwt_ref, gt_ref,
                y_ref, q_sc, acc_sc, o_sc, w_sc, s_sc, p_sc, *, tq, tk, topn):
    i = pl.program_id(2)
    nsub = tq // SUB_Q
    wsub = HEADS_PER_GROUP * SUB_Q
    cols = nsub * wsub
    nc = kc_ref.shape[1]

    def lanes(jq, hh):
        return slice(jq * wsub + hh * SUB_Q, jq * wsub + (hh + 1) * SUB_Q)

    def toks(jq):
        return slice(jq * SUB_Q, (jq + 1) * SUB_Q)

    for jq in range(nsub):
        for hh in range(HEADS_PER_GROUP):
            q_sc[0:HEAD_DIM, lanes(jq, hh)] = qt_ref[
                0, hh // 2, (hh % 2) * HEAD_DIM:(hh % 2 + 1) * HEAD_DIM, toks(jq)]
    q_sc[HEAD_DIM:LANES, :] = jnp.zeros((LANES - HEAD_DIM, cols), BF16)

    def tok(shape, lane0=0):
        lane = lane0 + lax.broadcasted_iota(jnp.int32, shape, 1)
        return i * tq + (lane // wsub) * SUB_Q + (lane & (SUB_Q - 1))

    nwin = WINDOW // SUB_Q
    kk = lax.broadcasted_iota(jnp.int32, (SUB_Q, wsub), 0)
    r = lax.broadcasted_iota(jnp.int32, (SUB_Q, wsub), 1) & (SUB_Q - 1)

    def chunk_start(jq, c):
        return pl.multiple_of(jnp.maximum(i * nsub + jq - nwin + c, 0) * SUB_Q, SUB_Q)

    def window_scores(jq):
        q_sub = q_sc[:, jq * wsub:(jq + 1) * wsub]
        for c in range(nwin + 1):
            sw = _dot(kw_ref[0, 0, pl.ds(chunk_start(jq, c), SUB_Q), :], q_sub)
            if c == nwin:
                sw = jnp.where(kk <= r, sw, NEG)
            else:
                if c == 0:
                    sw = jnp.where(kk > r, sw, NEG)
                sw = sw + jnp.where(i * nsub + jq - nwin + c >= 0, 0.0, NEG)
            w_sc[jq, c * SUB_Q:(c + 1) * SUB_Q, :] = sw

    def window_finish(jq):
        sw = w_sc[jq]
        pw = jnp.exp2(sw - jnp.max(sw, axis=0, keepdims=True)).astype(BF16)
        ow = jnp.zeros((LANES, wsub), F32)
        for c in range(nwin + 1):
            ow = ow + _dot(vwt_ref[0, 0, :, pl.ds(chunk_start(jq, c), SUB_Q)],
                           pw[c * SUB_Q:(c + 1) * SUB_Q])
        o_sc[1, :, jq * wsub:(jq + 1) * wsub] = ow

    s = _dot(kc_ref[0], q_sc[...])
    for jq in range(nsub):
        window_scores(jq)
    last_block = lax.shift_right_arithmetic(tok((1, cols)) - (CMP_LEN - 1), CMP_STRIDE.bit_length() - 1)
    s = jnp.where(lax.broadcasted_iota(jnp.int32, (nc, cols), 0) <= last_block, s, NEG)
    m = jnp.max(s, axis=0, keepdims=True)
    p = jnp.exp2(s - jnp.maximum(m, 0.5 * NEG))
    l = jnp.sum(p, axis=0, keepdims=True)
    p = p * (1.0 / jnp.where(l > 0.0, l, 1.0))
    o_sc[0] = _dot(vct_ref[0], p.astype(BF16))

    sums = []
    for jq in range(nsub):
        ps = p[:, lanes(jq, 0)]
        for hh in range(1, HEADS_PER_GROUP):
            ps = ps + p[:, lanes(jq, hh)]
        sums.append(ps)
    hi, lo = _split_bf16(jnp.concatenate(sums, axis=1))
    ov = ov_ref[...]
    imp = _dot(ov, hi) + _dot(ov, lo)

    for jq in range(nsub - 1):
        window_finish(jq)

    j = lax.broadcasted_iota(jnp.int32, (SEL_SLOTS, tq), 0)
    cur = (i * tq + lax.broadcasted_iota(jnp.int32, (SEL_SLOTS, tq), 1)) // SEL_LEN
    forced = (j == 0) | (j == cur) | (j == cur - 1)
    imp = jnp.where(j <= cur, jnp.where(forced, FORCED_SCORE, imp), -1.0)
    def ranked(n_rows):
        def run():
            sub = SUBLANES
            slabs = [imp[a:a + sub] for a in range(0, n_rows, sub)]
            ranks = [jnp.zeros((sub, tq), F32) for _ in slabs]
            sub_row = lax.broadcasted_iota(jnp.int32, (sub, tq), 0)
            for jj in range(n_rows):
                rowj = jnp.broadcast_to(imp[jj:jj + 1, :], (sub, tq))
                for a, xs in enumerate(slabs):
                    if a * sub > jj:
                        inc = jnp.where(rowj >= xs, 1.0, 0.0)
                    elif a * sub + sub - 1 <= jj:
                        inc = jnp.where(rowj > xs, 1.0, 0.0)
                    else:
                        tie = jnp.where(sub_row > jj - a * sub, 1.0, 0.0)
                        inc = jnp.where(rowj > xs, 1.0, jnp.where(rowj == xs, tie, 0.0))
                    ranks[a] = ranks[a] + inc
            rank = jnp.concatenate(ranks, axis=0)
            kept = jnp.where((rank < float(topn)) & (imp[0:n_rows] >= 0.0), 0.0, -1.0)
            if n_rows == SEL_SLOTS:
                return kept
            return jnp.concatenate([kept, jnp.full((SEL_SLOTS - n_rows, tq), -1.0, F32)], axis=0)
        return run

    row_options = tuple(r for r in (SEL_SLOTS // 2, 3 * SEL_SLOTS // 4) if r > topn) + (SEL_SLOTS,)
    visible = (i + 1) * tq // SEL_LEN
    branch = sum((visible > r).astype(jnp.int32) for r in (topn,) + row_options[:-1])
    flag = lax.switch(branch, [lambda: jnp.where(imp >= 0.0, 0.0, -1.0)]
                      + [ranked(r) for r in row_options]).astype(BF16)

    for jq in range(nsub):
        for hh in range(HEADS_PER_GROUP):
            q_sc[HEAD_DIM:LANES, lanes(jq, hh)] = flag[:, toks(jq)]
    acc_sc[...] = jnp.zeros(acc_sc.shape, F32)

    blocks = [slice(a, a + MXU_COLS) for a in range(0, cols, MXU_COLS)]

    def key_tile(kt):
        return ks_ref[0, 0, pl.ds(pl.multiple_of(kt * tk, tk), tk), :]

    def value_tile(kt):
        return vst_ref[0, 0, :, pl.ds(pl.multiple_of(jnp.maximum(kt, 0) * tk, tk), tk)]

    def score_block(k, slot, cs):
        ss = _dot(k, q_sc[:, cs])
        s_sc[slot, :, cs] = ss
        return jnp.max(ss, axis=0, keepdims=True)

    def softmax_block(slot, cs, m_prev, m_blk, diagonal):
        ss = s_sc[slot, :, cs]
        if diagonal:
            m_blk = jnp.max(ss, axis=0, keepdims=True)
        m_new = jnp.maximum(m_prev, m_blk)
        p_sc[slot, :, cs] = jnp.exp2(ss - m_new).astype(BF16)
        return m_new, jnp.exp2(m_prev - m_new)

    def value_block(vt, slot, cs, a_blk):
        acc_sc[:, cs] = acc_sc[:, cs] * a_blk + _dot(vt, p_sc[slot, :, cs])

    def row(parts):
        return jnp.concatenate(parts, axis=1)

    n_tiles = (i * tq) // tk + 1
    k_first = key_tile(0)
    m_first = row([score_block(k_first, 0, cs) for cs in blocks])
    p_sc[1] = jnp.zeros(p_sc.shape[1:], BF16)
    window_finish(nsub - 1)

    def steady(k, carry):
        m_run, alpha, m_tile = carry

        def arm(slot):
            kx, vt = key_tile(k), value_tile(k - 2)
            m_new, a_new, m_next = [], [], []
            for cs in blocks:
                mn, an = softmax_block(1 - slot, cs, m_run[:, cs], m_tile[:, cs], False)
                m_next.append(score_block(kx, slot, cs))
                value_block(vt, slot, cs, alpha[:, cs])
                m_new.append(mn)
                a_new.append(an)
            return row(m_new), row(a_new), row(m_next)

        return lax.cond(k % 2 == 0, lambda: arm(0), lambda: arm(1))

    m_run, alpha, m_tile = lax.fori_loop(
        1, n_tiles, steady,
        (jnp.full((1, cols), NEG, F32), jnp.ones((1, cols), F32), m_first))

    def drain(slot):
        row0 = i * tq - (n_tiles - 1) * tk
        vt_prev, vt_last = value_tile(n_tiles - 2), value_tile(n_tiles - 1)
        for cs in blocks:
            value_block(vt_prev, 1 - slot, cs, alpha[:, cs])
        for jq in range(nsub):
            rows = pl.ds(pl.multiple_of(row0 + jq * SUB_Q, SUB_Q), SUB_Q)
            own = slice(jq * wsub, (jq + 1) * wsub)
            s_sc[slot, rows, own] = jnp.where(kk <= r, s_sc[slot, rows, own], NEG)
        for cs in blocks:
            _, a_last = softmax_block(slot, cs, m_run[:, cs], m_tile[:, cs], True)
            value_block(vt_last, slot, cs, a_last)

    @pl.when((n_tiles - 1) % 2 == 0)
    def _():
        drain(0)

    @pl.when((n_tiles - 1) % 2 == 1)
    def _():
        drain(1)

    gt = gt_ref[0]
    tiles = []
    for jq in range(nsub):
        heads = []
        for hh in range(HEADS_PER_GROUP):
            cs = lanes(jq, hh)
            g_c = gt[hh:hh + 1, toks(jq)]
            g_s = gt[HEADS_PER_GROUP + hh:HEADS_PER_GROUP + hh + 1, toks(jq)]
            g_w = gt[2 * HEADS_PER_GROUP + hh:2 * HEADS_PER_GROUP + hh + 1, toks(jq)]
            heads.append(o_sc[0, 0:HEAD_DIM, cs] * g_c
                         + acc_sc[0:HEAD_DIM, cs] * (g_s / acc_sc[HEAD_DIM:HEAD_DIM + 1, cs])
                         + o_sc[1, 0:HEAD_DIM, cs] * (g_w / o_sc[1, HEAD_DIM:HEAD_DIM + 1, cs]))
        tiles.append(jnp.concatenate(heads, axis=0).T)
    y_ref[0] = jnp.concatenate(tiles, axis=0).astype(BF16)


def _nsa(qt, kcmp, vcmpt, ov, ks, vst, kw, vwt, gt, tq, tk, topn):
    B, _, _, S = qt.shape
    nc = kcmp.shape[1]
    cols = HEADS_PER_GROUP * tq
    kern = functools.partial(_nsa_kernel, tq=tq, tk=tk, topn=topn)
    seq_rows = pl.BlockSpec((1, 1, S, LANES), lambda b, g, i: (b, g, 0, 0))
    seq_cols = pl.BlockSpec((1, 1, LANES, S), lambda b, g, i: (b, g, 0, 0))
    return pl.pallas_call(
        kern,
        grid=(B, KV_GROUPS, S // tq),
        in_specs=[pl.BlockSpec((1, HEADS_PER_GROUP // 2, LANES, tq), lambda b, g, i: (b, g, 0, i)),
                  pl.BlockSpec((1, nc, LANES), lambda b, g, i: (b * KV_GROUPS + g, 0, 0)),
                  pl.BlockSpec((1, LANES, nc), lambda b, g, i: (b * KV_GROUPS + g, 0, 0)),
                  pl.BlockSpec((SEL_SLOTS, nc), lambda b, g, i: (0, 0)),
                  seq_rows, seq_cols, seq_rows, seq_cols,
                  pl.BlockSpec((1, GATE_ROWS, tq), lambda b, g, i: (b, g, i))],
        out_specs=pl.BlockSpec((1, tq, HEADS_PER_GROUP * HEAD_DIM), lambda b, g, i: (b, i, g)),
        out_shape=jax.ShapeDtypeStruct((B, S, N_HEADS * HEAD_DIM), BF16),
        scratch_shapes=[pltpu.VMEM((LANES, cols), BF16), pltpu.VMEM((LANES, cols), F32),
                        pltpu.VMEM((2, LANES, cols), F32),
                        pltpu.VMEM((tq // SUB_Q, WINDOW + SUB_Q, HEADS_PER_GROUP * SUB_Q), F32),
                        pltpu.VMEM((2, tk, cols), F32), pltpu.VMEM((2, tk, cols), BF16)],
        compiler_params=pltpu.CompilerParams(vmem_limit_bytes=VMEM_LIMIT),
    )(qt, kcmp, vcmpt, ov, ks, vst, kw, vwt, gt)


def _out_ffn_kernel(x_ref, mod_ref, n2_ref, yp_ref, yn_ref, wo_ref, w1_ref, w2_ref, o_ref, *, ff_chunk):
    x = x_ref[0]
    gate1 = mod_ref[0, 0, 2:3, :]
    shift2 = mod_ref[0, 0, 3:4, :]
    scale2 = mod_ref[0, 0, 4:5, :]
    gate2 = mod_ref[0, 0, 5:6, :]
    half = wo_ref.shape[0] // 2
    mix = _dot(yp_ref[0], wo_ref[0:half, :]) + _dot(yn_ref[0], wo_ref[half:2 * half, :])
    x1 = x + gate1 * mix
    ms = jnp.mean(x1 * x1, axis=-1, keepdims=True)
    h2 = ((x1 * lax.rsqrt(ms + EPS) * n2_ref[...]) * (1.0 + scale2) + shift2).astype(BF16)
    d_ff = w1_ref.shape[1]
    ff = jnp.zeros(x.shape, F32)
    for c in range(d_ff // ff_chunk):
        a = jnp.maximum(_dot(h2, w1_ref[:, c * ff_chunk:(c + 1) * ff_chunk]), 0.0)
        ff = ff + _dot((a * a).astype(BF16), w2_ref[c * ff_chunk:(c + 1) * ff_chunk, :])
    o_ref[0] = x1 + gate2 * ff


def _out_ffn(x, mod_l, layer, n2, yp, yn, wo, w1, w2, tm):
    B, S, D = x.shape
    kern = functools.partial(_out_ffn_kernel, ff_chunk=1024)
    tok = lambda width: pl.BlockSpec((1, tm, width), lambda b, i: (b, i, 0))
    const = lambda shape: pl.BlockSpec(shape, lambda b, i: (0,) * len(shape),
                                       pipeline_mode=pl.Buffered(1))
    return pl.pallas_call(
        kern,
        grid=(B, S // tm),
        in_specs=[tok(D),
                  pl.BlockSpec((1, 1, 6, D), lambda b, i: (layer, b, 0, 0)),
                  const((1, D)),
                  tok(512), tok(512),
                  const(wo.shape), const(w1.shape), const(w2.shape)],
        out_specs=tok(D),
        out_shape=jax.ShapeDtypeStruct((B, S, D), F32),
        compiler_params=pltpu.CompilerParams(vmem_limit_bytes=VMEM_LIMIT),
    )(x, mod_l, n2, yp, yn, wo, w1, w2)


def _rot_cols(w):
    shp = w.shape
    w4 = w.reshape(shp[:-1] + (shp[-1] // HEAD_DIM, 2, HEAD_DIM // 2))
    return jnp.stack([-w4[..., 1, :], w4[..., 0, :]], axis=-2).reshape(shp)


def _swap_halves(g):
    half = HEAD_DIM // 2
    return jnp.concatenate([g[..., half:], g[..., :half]], axis=-1)


def _rope_tables(pos, reps, signed):
    half = HEAD_DIM // 2
    inv = ROPE_THETA ** (-jnp.arange(half, dtype=F32) * 2.0 / HEAD_DIM)
    ang = pos.astype(F32)[:, None] * inv[None, :]
    cos = jnp.tile(jnp.cos(ang), (1, 2 * reps))
    sin = jnp.sin(ang)
    sin = jnp.tile(jnp.concatenate([-sin if signed else sin, sin], axis=1), (1, reps))
    return cos, sin


def _overlap(n_chunks, n_sel):
    cs0 = np.arange(n_chunks) * CMP_STRIDE
    ss0 = np.arange(n_sel) * SEL_LEN
    ov = np.minimum(cs0[:, None] + CMP_LEN, ss0[None, :] + SEL_LEN) - np.maximum(cs0[:, None], ss0[None, :])
    ov = np.clip(ov, 0, None).astype(np.float32) / CMP_LEN
    out = np.zeros((SEL_SLOTS, n_chunks), np.float32)
    out[:n_sel] = ov.T
    return out


def _gate_cols(w_g):
    idx = np.zeros((LANES,), np.int32)
    used = np.zeros((LANES,), np.float32)
    for g in range(KV_GROUPS):
        for br in range(N_BRANCH):
            for hg in range(HEADS_PER_GROUP):
                c = g * GATE_ROWS + br * HEADS_PER_GROUP + hg
                idx[c] = (g * HEADS_PER_GROUP + hg) * N_BRANCH + br
                used[c] = 1.0
    return w_g[..., idx] * used


def kernel(x, c, w_mod, b_mod, norm1, norm2, w_in, pool_w, pool_scale, q_norm, k_norm, cmp_pe, cmp_w1, cmp_w2, w_out, w_ff1, w_ff2):
    B, S, D = x.shape
    depth = w_mod.shape[0]
    n_chunks = S // CMP_STRIDE
    n_sel = S // SEL_LEN
    assert D == 1024 and S % 2048 == 0 and n_sel <= SEL_SLOTS
    topn = min(SEL_TOPN, n_sel)
    tm_in, tm_out, tq, tk = 512, 512, 512, 512

    mod = _modulation(c, w_mod, b_mod)

    w_b = w_in.astype(BF16)
    w_g = _gate_cols(w_in[:, :, C_END:]).astype(BF16)
    gq = jnp.tile(q_norm, (1, N_HEADS))[:, None, :]
    gk = jnp.stack([jnp.tile(k_norm[:, 1], (1, 2)), jnp.tile(k_norm[:, 2], (1, 2))], axis=1)
    pad64 = ((0, 0), (0, HEAD_DIM))
    kn = jnp.stack([jnp.pad(k_norm[:, 0], pad64), jnp.pad(_swap_halves(k_norm[:, 0]), pad64)], axis=1)
    w1_k = jnp.pad(cmp_w1[:, 0].astype(BF16), ((0, 0), (0, 0), (0, HEAD_DIM)))
    w1_v = jnp.pad(cmp_w1[:, 1].astype(BF16), ((0, 0), (0, 0), (HEAD_DIM, 0)))
    w1c = jnp.stack([w1_k.reshape(depth, CMP_LEN, HEAD_DIM, LANES),
                     w1_v.reshape(depth, CMP_LEN, HEAD_DIM, LANES)],
                    axis=2).reshape(depth, CMP_LEN, LANES, LANES)
    zero_w2 = jnp.zeros_like(cmp_w2[:, 0])
    w2c = jnp.concatenate([
        jnp.concatenate([cmp_w2[:, 0], zero_w2, _rot_cols(cmp_w2[:, 0]), zero_w2, zero_w2, zero_w2], axis=-1),
        jnp.concatenate([zero_w2, zero_w2, zero_w2, zero_w2, cmp_w2[:, 1], zero_w2], axis=-1)],
        axis=-2).astype(BF16)
    pe = jnp.concatenate([cmp_pe[:, 0], cmp_pe[:, 1]], axis=-1)
    pool_wb = pool_w.astype(BF16)
    wo_b, w1_b, w2_b = w_out.astype(BF16), w_ff1.astype(BF16), w_ff2.astype(BF16)

    cos, sin = _rope_tables(jnp.arange(S), 2, signed=True)
    cos_c, sin_c = _rope_tables(jnp.arange(n_chunks) * CMP_STRIDE + (CMP_LEN - 1), 2, signed=False)
    bd = jnp.asarray(np.kron(np.eye(N_HEADS, dtype=np.float32),
                             np.full((HEAD_DIM, HEAD_DIM), 1.0 / HEAD_DIM, np.float32)), BF16)
    ov = jnp.asarray(_overlap(n_chunks, n_sel), BF16)

    for l in range(depth):
        y_pool, qt, ks, kw, vst, vwt, kvc, gt = _inproj(
            x, mod, l, norm1[:, None, :], w_b, w_g, cos, sin, gq, gk, bd,
            pool_wb, pool_scale[:, None, :], tm_in)
        kcmp, vcmpt = _compress(kvc, l, pe, w1c, w2c, kn, cos_c, sin_c)
        y_nsa = _nsa(qt, kcmp, vcmpt, ov, ks, vst, kw, vwt, gt, tq, tk, topn)
        x = _out_ffn(x, mod, l, norm2[l:l + 1], y_pool, y_nsa, wo_b[l], w1_b[l], w2_b[l], tm_out)
    return x
```

```python
import functools
import math

import numpy as np
import jax
import jax.numpy as jnp
from jax import lax
from jax.experimental import pallas as pl
from jax.experimental.pallas import tpu as pltpu

F32 = jnp.float32
BF16 = jnp.bfloat16

LANES = 128
SUBLANES = 8
HEAD_DIM = 64
N_HEADS = 8
KV_GROUPS = 2
HEADS_PER_GROUP = N_HEADS // KV_GROUPS
N_BRANCH = 3
GATE_ROWS = 16
POOL_WINDOWS = (2, 4, 8, 16)
POOL_HALO = 16
CMP_LEN = 32
CMP_STRIDE = 16
SEL_LEN = 64
SEL_SLOTS = 64
SEL_TOPN = 16
SUB_Q = 128
WINDOW = 512
ROPE_THETA = 10000.0
EPS = 1e-6
NEG = -1e30
FORCED_SCORE = 1e4
LOG2E = math.log2(math.e)
SCALE = 1.0 / math.sqrt(HEAD_DIM)
VMEM_LIMIT = 56 * 1024 * 1024
MXU_COLS = 256


def _dot(a, b):
    return jnp.dot(a, b, preferred_element_type=F32)


def _split_bf16(v):
    hi = v.astype(BF16)
    lo = (v - hi.astype(F32)).astype(BF16)
    return hi, lo


def _mod_kernel(c_ref, w_ref, b_ref, o_ref, *, batch):
    c = c_ref[...]
    rows = c.shape[0]
    ca = c * jax.nn.sigmoid(c)
    hi = ca.astype(BF16).astype(F32)
    row = lax.broadcasted_iota(jnp.int32, ca.shape, 0)
    a = jnp.where(row < batch, hi, ca - hi).astype(BF16)
    w = w_ref[0]
    w_hi, w_lo = _split_bf16(w)
    r1 = _dot(a, w_hi)
    r2 = _dot(a, w_lo)
    o_ref[0] = r1 + pltpu.roll(r1, rows - batch, axis=0) + r2 + b_ref[0]


def _modulation(c, w_mod, b_mod):
    depth, d, n = w_mod.shape
    batch = c.shape[0]
    rows = -(-2 * batch // SUBLANES) * SUBLANES
    c2 = jnp.concatenate([c, c, jnp.zeros((rows - 2 * batch, d), F32)], axis=0)
    tn = 1536
    out = pl.pallas_call(
        functools.partial(_mod_kernel, batch=batch),
        grid=(depth, n // tn),
        in_specs=[pl.BlockSpec((rows, d), lambda l, j: (0, 0)),
                  pl.BlockSpec((1, d, tn), lambda l, j: (l, 0, j)),
                  pl.BlockSpec((1, 1, tn), lambda l, j: (l, 0, j))],
        out_specs=pl.BlockSpec((1, rows, tn), lambda l, j: (l, 0, j)),
        out_shape=jax.ShapeDtypeStruct((depth, rows, n), F32),
        compiler_params=pltpu.CompilerParams(vmem_limit_bytes=VMEM_LIMIT),
    )(c2, w_mod, b_mod.reshape(depth, 1, n))
    return out[:, :batch].reshape(depth, batch, 6, d)


C_U = 0
C_Q = 512
C_KV = 1024
C_END = 1792


def _inproj_kernel(x_ref, mod_ref, n1_ref, w_ref, wg_ref, cos_ref, sin_ref, gq_ref, gk_ref, bd_ref,
                   pw_ref, psc_ref,
                   yp_ref, qt_ref, ks_ref, kw_ref, vst_ref, vwt_ref, kvc_ref, gt_ref, ext_sc, *, tm):
    i = pl.program_id(1)
    x = x_ref[0]
    shift = mod_ref[0, 0, 0:1, :]
    scale = mod_ref[0, 0, 1:2, :]
    ms = jnp.mean(x * x, axis=-1, keepdims=True)
    h = (x * lax.rsqrt(ms + EPS) * n1_ref[...]) * (1.0 + scale) + shift
    hb = h.astype(BF16)

    def proj(a, b):
        return _dot(hb, w_ref[:, a:b])

    def seg_mean(v, bdm):
        return _dot((v * v).astype(BF16), bdm)

    def rope(y, cos_t, sin_t):
        width = y.shape[1]
        first = (lax.broadcasted_iota(jnp.int32, y.shape, 1) & (HEAD_DIM // 2)) == 0
        rot = jnp.where(first, pltpu.roll(y, width - HEAD_DIM // 2, axis=1),
                        pltpu.roll(y, HEAD_DIM // 2, axis=1))
        return y * cos_t + rot * sin_t

    @pl.when(i == 0)
    def _():
        ext_sc[0:POOL_HALO, :] = jnp.zeros((POOL_HALO, C_Q), F32)

    @pl.when(i > 0)
    def _():
        ext_sc[0:POOL_HALO, :] = ext_sc[tm:tm + POOL_HALO, :]

    ext_sc[POOL_HALO:POOL_HALO + tm, :] = proj(C_U, C_Q)
    q = proj(C_Q, C_KV)
    kv = proj(C_KV, C_END)
    gates = _dot(hb, wg_ref[...])
    pos = i * tm + lax.broadcasted_iota(jnp.int32, (tm, LANES), 0)
    for g, w in enumerate(POOL_WINDOWS):
        c0, c1 = g * LANES, (g + 1) * LANES
        cur = ext_sc[POOL_HALO:POOL_HALO + tm, c0:c1]
        acc = cur
        for k in range(1, w):
            acc = acc + ext_sc[POOL_HALO - k:POOL_HALO - k + tm, c0:c1]
        cnt = jnp.minimum(pos + 1, w).astype(F32)
        pooled = acc / cnt - cur
        yp_ref[0, :, c0:c1] = (_dot(pooled.astype(BF16), pw_ref[g]) * psc_ref[:, c0:c1]).astype(BF16)

    cos = cos_ref[...]
    sin = sin_ref[...]
    cos4 = jnp.concatenate([cos] * 4, axis=1)
    sin4 = jnp.concatenate([sin] * 4, axis=1)
    rq = lax.rsqrt(seg_mean(q, bd_ref[...]) + EPS)
    qo = rope(q * gq_ref[...], cos4, sin4) * (rq * (SCALE * LOG2E))
    for hp in range(N_HEADS // 2):
        qt_ref[0, hp] = qo[:, LANES * hp:LANES * (hp + 1)].T.astype(BF16)

    bd2 =bd_ref[0:LANES, 0:LANES]
    k_c, k_s, k_w = kv[:, 0:128], kv[:, 128:256], kv[:, 256:384]
    v_c, v_s, v_w = kv[:, 384:512], kv[:, 512:640], kv[:, 640:768]
    kso = rope(k_s * gk_ref[0:1, :], cos, sin) * lax.rsqrt(seg_mean(k_s, bd2) + EPS)
    kwo = rope(k_w * gk_ref[1:2, :], cos, sin) * lax.rsqrt(seg_mean(k_w, bd2) + EPS)

    lane = lax.broadcasted_iota(jnp.int32, (tm, LANES), 1)
    low = lane < HEAD_DIM
    pos = i * tm + lax.broadcasted_iota(jnp.int32, (tm, LANES), 0)
    blk = pos // SEL_LEN
    blk_flag = jnp.where(lane - HEAD_DIM == blk, -NEG, 0.0)
    ones_col = jnp.where(lane == HEAD_DIM, 1.0, 0.0)
    for g in range(KV_GROUPS):
        def grp(a, g=g):
            return pltpu.roll(a, HEAD_DIM, axis=1) if g else a
        ks_ref[0, g] = jnp.where(low, grp(kso), blk_flag).astype(BF16)
        kw_ref[0, g] = jnp.where(low, grp(kwo), 0.0).astype(BF16)
        vst_ref[0, g] = jnp.where(low, grp(v_s), ones_col).T.astype(BF16)
        vwt_ref[0, g] = jnp.where(low, grp(v_w), ones_col).T.astype(BF16)
        kvc_ref[0, g] = jnp.where(low, grp(k_c), v_c if g else pltpu.roll(v_c, HEAD_DIM, axis=1))
    gt_ref[0] = jax.nn.sigmoid(gates).T[0:KV_GROUPS * GATE_ROWS]


def _inproj(x, mod, layer, n1, w, wg, cos, sin, gq, gk, bd, pw, psc, tm):
    B, S, D = x.shape
    kern = functools.partial(_inproj_kernel, tm=tm)
    grp_rows = lambda width: pl.BlockSpec((1, KV_GROUPS, tm, width), lambda b, i: (b, 0, i, 0))
    grp_cols = pl.BlockSpec((1, KV_GROUPS, LANES, tm), lambda b, i: (b, 0, 0, i))
    rows_shape = lambda width: jax.ShapeDtypeStruct((B, KV_GROUPS, S, width), BF16)
    cols_shape = jax.ShapeDtypeStruct((B, KV_GROUPS, LANES, S), BF16)
    const = lambda shape: pl.BlockSpec(shape, lambda b, i: (0,) * len(shape))
    per_layer = lambda a: pl.BlockSpec((None,) + a.shape[1:], lambda b, i: (layer,) + (0,) * (a.ndim - 1))
    return pl.pallas_call(
        kern,
        grid=(B, S // tm),
        in_specs=[pl.BlockSpec((1, tm, D), lambda b, i: (b, i, 0)),
                  pl.BlockSpec((1, 1, 6, D), lambda b, i: (layer, b, 0, 0)),
                  per_layer(n1), per_layer(w), per_layer(wg),
                  pl.BlockSpec((tm, LANES), lambda b, i: (i, 0)),
                  pl.BlockSpec((tm, LANES), lambda b, i: (i, 0)),
                  per_layer(gq), per_layer(gk), const(bd.shape),
                  per_layer(pw), per_layer(psc)],
        out_specs=[pl.BlockSpec((1, tm, C_Q), lambda b, i: (b, i, 0)),
                   pl.BlockSpec((1, N_HEADS // 2, LANES, tm), lambda b, i: (b, 0, 0, i)),
                   grp_rows(LANES), grp_rows(LANES), grp_cols, grp_cols,
                   grp_rows(LANES),
                   pl.BlockSpec((1, KV_GROUPS * GATE_ROWS, tm), lambda b, i: (b, 0, i))],
        out_shape=[jax.ShapeDtypeStruct((B, S, C_Q), BF16),
                   jax.ShapeDtypeStruct((B, N_HEADS // 2, LANES, S), BF16),
                   rows_shape(LANES), rows_shape(LANES), cols_shape, cols_shape,
                   jax.ShapeDtypeStruct((B, KV_GROUPS, S, LANES), F32),
                   jax.ShapeDtypeStruct((B, KV_GROUPS * GATE_ROWS, S), F32)],
        scratch_shapes=[pltpu.VMEM((tm + POOL_HALO, C_Q), F32)],
        compiler_params=pltpu.CompilerParams(vmem_limit_bytes=VMEM_LIMIT),
    )(x, mod, n1, w, wg, cos, sin, gq, gk, bd, pw, psc)


def _compress_kernel(x_ref, pe_ref, w1_ref, w2_ref, kn_ref, cos_ref, sin_ref, ko_ref, vot_ref):
    nc = ko_ref.shape[1]
    taps = CMP_LEN // 2
    first = jnp.zeros((nc, LANES), F32)
    second = jnp.zeros((nc, LANES), F32)
    for t in range(taps):
        x = x_ref[0, 0, pl.ds(t, nc, stride=CMP_STRIDE), :]
        first = first + _dot((x + pe_ref[t:t + 1, :]).astype(BF16), w1_ref[t])
        second = second + _dot((x + pe_ref[taps + t:taps + t + 1, :]).astype(BF16), w1_ref[taps + t])
    pre = first + pltpu.roll(second, nc - 1, axis=0)
    h = (pre * jax.nn.sigmoid(pre)).astype(BF16)
    kkv = _dot(h, w2_ref[...])
    k = kkv[:, 0:LANES]
    kr = kkv[:, LANES:2 * LANES]
    r = lax.rsqrt(jnp.sum(k * k, axis=-1, keepdims=True) * (1.0 / HEAD_DIM) + EPS)
    ko = (k * kn_ref[0:1, :] * cos_ref[...] + kr * kn_ref[1:2, :] * sin_ref[...]) * r
    ko_ref[0] = ko.astype(BF16)
    vot_ref[0] = kkv[:, 2 * LANES:3 * LANES].T.astype(BF16)


def _compress(kvc, layer, pe, w1, w2, kn, cos_c, sin_c):
    B, G, S, _ = kvc.shape
    nc = S // CMP_STRIDE
    const = lambda shape: pl.BlockSpec(shape, lambda b, g: (0,) * len(shape))
    per_layer = lambda a: pl.BlockSpec((None,) + a.shape[1:], lambda b, g: (layer,) + (0,) * (a.ndim - 1))
    return pl.pallas_call(
        _compress_kernel,
        grid=(B, G),
        in_specs=[pl.BlockSpec((1, 1, S, LANES), lambda b, g: (b, g, 0, 0)),
                  per_layer(pe), per_layer(w1), per_layer(w2), per_layer(kn),
                  const(cos_c.shape), const(sin_c.shape)],
        out_specs=[pl.BlockSpec((1, nc, LANES), lambda b, g: (b * G + g, 0, 0)),
                   pl.BlockSpec((1, LANES, nc), lambda b, g: (b * G + g, 0, 0))],
        out_shape=[jax.ShapeDtypeStruct((B * G, nc, LANES), BF16),
                   jax.ShapeDtypeStruct((B * G, LANES, nc), BF16)],
        compiler_params=pltpu.CompilerParams(vmem_limit_bytes=VMEM_LIMIT),
    )(kvc, pe, w1, w2, kn, cos_c, sin_c)


def _nsa_kernel(qt_ref, kc_ref, vct_ref, ov_ref, ks_ref, vst_ref, kw_ref, vwt_ref, gt_ref,
                y_ref, q_sc, acc_sc, o_sc, w_sc, s_sc, p_sc, *, tq, tk, topn):
    i = pl.program_id(2)
    nsub = tq // SUB_Q
    wsub = HEADS_PER_GROUP * SUB_Q
    cols = nsub * wsub
    nc = kc_ref.shape[1]

    def lanes(jq, hh):
        return slice(jq * wsub + hh * SUB_Q, jq * wsub + (hh + 1) * SUB_Q)

    def toks(jq):
        return slice(jq * SUB_Q, (jq + 1) * SUB_Q)

    for jq in range(nsub):
        for hh in range(HEADS_PER_GROUP):
            q_sc[0:HEAD_DIM, lanes(jq, hh)] = qt_ref[
                0, hh // 2, (hh % 2) * HEAD_DIM:(hh % 2 + 1) * HEAD_DIM, toks(jq)]
    q_sc[HEAD_DIM:LANES, :] = jnp.zeros((LANES - HEAD_DIM, cols), BF16)

    def tok(shape, lane0=0):
        lane = lane0 + lax.broadcasted_iota(jnp.int32, shape, 1)
        return i * tq + (lane // wsub) * SUB_Q + (lane & (SUB_Q - 1))

    nwin = WINDOW // SUB_Q
    kk = lax.broadcasted_iota(jnp.int32, (SUB_Q, wsub), 0)
    r = lax.broadcasted_iota(jnp.int32, (SUB_Q, wsub), 1) & (SUB_Q - 1)

    def chunk_start(jq, c):
        return pl.multiple_of(jnp.maximum(i * nsub + jq - nwin + c, 0) * SUB_Q, SUB_Q)

    def window_scores(jq):
        q_sub = q_sc[:, jq * wsub:(jq + 1) * wsub]
        for c in range(nwin + 1):
            sw = _dot(kw_ref[0, 0, pl.ds(chunk_start(jq, c), SUB_Q), :], q_sub)
            if c == nwin:
                sw = jnp.where(kk <= r, sw, NEG)
            else:
                if c == 0:
                    sw = jnp.where(kk > r, sw, NEG)
                sw = sw + jnp.where(i * nsub + jq - nwin + c >= 0, 0.0, NEG)
            w_sc[jq, c * SUB_Q:(c + 1) * SUB_Q, :] = sw

    def window_finish(jq):
        sw = w_sc[jq]
        pw = jnp.exp2(sw - jnp.max(sw, axis=0, keepdims=True)).astype(BF16)
        ow = jnp.zeros((LANES, wsub), F32)
        for c in range(nwin + 1):
            ow = ow + _dot(vwt_ref[0, 0, :, pl.ds(chunk_start(jq, c), SUB_Q)],
                           pw[c * SUB_Q:(c + 1) * SUB_Q])
        o_sc[1, :, jq * wsub:(jq + 1) * wsub] = ow

    s = _dot(kc_ref[0], q_sc[...])
    for jq in range(nsub):
        window_scores(jq)
    last_block = lax.shift_right_arithmetic(tok((1, cols)) - (CMP_LEN - 1), CMP_STRIDE.bit_length() - 1)
    s = jnp.where(lax.broadcasted_iota(jnp.int32, (nc, cols), 0) <= last_block, s, NEG)
    m = jnp.max(s, axis=0, keepdims=True)
    p = jnp.exp2(s - jnp.maximum(m, 0.5 * NEG))
    l = jnp.sum(p, axis=0, keepdims=True)
    inv_l = 1.0 / jnp.where(l > 0.0, l, 1.0)
    o_sc[0] = _dot(vct_ref[0], p.astype(BF16)) * inv_l

    sums = []
    for jq in range(nsub):
        ps = p[:, lanes(jq, 0)] * inv_l[:, lanes(jq, 0)]
        for hh in range(1, HEADS_PER_GROUP):
            ps = ps + p[:, lanes(jq, hh)] * inv_l[:, lanes(jq, hh)]
        sums.append(ps)
    hi, lo = _split_bf16(jnp.concatenate(sums, axis=1))
    ov = ov_ref[...]
    imp = _dot(ov, hi) + _dot(ov, lo)

    for jq in range(nsub - 1):
        window_finish(jq)

    j = lax.broadcasted_iota(jnp.int32, (SEL_SLOTS, tq), 0)
    cur = (i * tq + lax.broadcasted_iota(jnp.int32, (SEL_SLOTS, tq), 1)) // SEL_LEN
    forced = (j == 0) | (j == cur) | (j == cur - 1)
    imp = jnp.where(j <= cur, jnp.where(forced, FORCED_SCORE, imp), -1.0)
    def ranked(n_rows):
        def run():
            sub = SUBLANES
            slabs = [imp[a:a + sub] for a in range(0, n_rows, sub)]
            ranks = [jnp.zeros((sub, tq), F32) for _ in slabs]
            sub_row = lax.broadcasted_iota(jnp.int32, (sub, tq), 0)
            for jj in range(n_rows):
                rowj = jnp.broadcast_to(imp[jj:jj + 1, :], (sub, tq))
                for a, xs in enumerate(slabs):
                    if a * sub > jj:
                        inc = jnp.where(rowj >= xs, 1.0, 0.0)
                    elif a * sub + sub - 1 <= jj:
                        inc = jnp.where(rowj > xs, 1.0, 0.0)
                    else:
                        tie = jnp.where(sub_row > jj - a * sub, 1.0, 0.0)
                        inc = jnp.where(rowj > xs, 1.0, jnp.where(rowj == xs, tie, 0.0))
                    ranks[a] = ranks[a] + inc
            rank = jnp.concatenate(ranks, axis=0)
            kept = jnp.where((rank < float(topn)) & (imp[0:n_rows] >= 0.0), 0.0, -1.0)
            if n_rows == SEL_SLOTS:
                return kept
            return jnp.concatenate([kept, jnp.full((SEL_SLOTS - n_rows, tq), -1.0, F32)], axis=0)
        return run

    row_options = tuple(r for r in (SEL_SLOTS // 2, 3 * SEL_SLOTS // 4) if r > topn) + (SEL_SLOTS,)
    visible = (i + 1) * tq // SEL_LEN
    branch = sum((visible > r).astype(jnp.int32) for r in (topn,) + row_options[:-1])
    flag = lax.switch(branch, [lambda: jnp.where(imp >= 0.0, 0.0, -1.0)]
                      + [ranked(r) for r in row_options]).astype(BF16)

    for jq in range(nsub):
        for hh in range(HEADS_PER_GROUP):
            q_sc[HEAD_DIM:LANES, lanes(jq, hh)] = flag[:, toks(jq)]
    acc_sc[...] = jnp.zeros(acc_sc.shape, F32)

    blocks = [slice(a, a + MXU_COLS) for a in range(0, cols, MXU_COLS)]

    def key_tile(kt):
        return ks_ref[0, 0, pl.ds(pl.multiple_of(kt * tk, tk), tk), :]

    def value_tile(kt):
        return vst_ref[0, 0, :, pl.ds(pl.multiple_of(jnp.maximum(kt, 0) * tk, tk), tk)]

    def score_block(k, slot, cs):
        ss = _dot(k, q_sc[:, cs])
        s_sc[slot, :, cs] = ss
        return jnp.max(ss, axis=0, keepdims=True)

    def softmax_block(slot, cs, m_prev, m_blk, diagonal):
        ss = s_sc[slot, :, cs]
        if diagonal:
            m_blk = jnp.max(ss, axis=0, keepdims=True)
        m_new = jnp.maximum(m_prev, m_blk)
        p_sc[slot, :, cs] = jnp.exp2(ss - m_new).astype(BF16)
        return m_new, jnp.exp2(m_prev - m_new)

    def value_block(vt, slot, cs, a_blk):
        acc_sc[:, cs] = acc_sc[:, cs] * a_blk + _dot(vt, p_sc[slot, :, cs])

    def row(parts):
        return jnp.concatenate(parts, axis=1)

    n_tiles = (i * tq) // tk + 1
    k_first = key_tile(0)
    m_first = row([score_block(k_first, 0, cs) for cs in blocks])
    p_sc[1] = jnp.zeros(p_sc.shape[1:], BF16)
    window_finish(nsub - 1)

    def steady(k, carry):
        m_run, alpha, m_tile = carry

        def arm(slot):
            kx, vt = key_tile(k), value_tile(k - 2)
            m_new, a_new, m_next = [], [], []
            for cs in blocks:
                mn, an = softmax_block(1 - slot, cs, m_run[:, cs], m_tile[:, cs], False)
                m_next.append(score_block(kx, slot, cs))
                value_block(vt, slot, cs, alpha[:, cs])
                m_new.append(mn)
                a_new.append(an)
            return row(m_new), row(a_new), row(m_next)

        return lax.cond(k % 2 == 0, lambda: arm(0), lambda: arm(1))

    m_run, alpha, m_tile = lax.fori_loop(
        1, n_tiles, steady,
        (jnp.full((1, cols), NEG, F32), jnp.ones((1, cols), F32), m_first))

    def drain(slot):
        row0 = i * tq - (n_tiles - 1) * tk
        vt_prev, vt_last = value_tile(n_tiles - 2), value_tile(n_tiles - 1)
        for cs in blocks:
            value_block(vt_prev, 1 - slot, cs, alpha[:, cs])
        for jq in range(nsub):
            rows = pl.ds(pl.multiple_of(row0 + jq * SUB_Q, SUB_Q), SUB_Q)
            own = slice(jq * wsub, (jq + 1) * wsub)
            s_sc[slot, rows, own] = jnp.where(kk <= r, s_sc[slot, rows, own], NEG)
        for cs in blocks:
            _, a_last = softmax_block(slot, cs, m_run[:, cs], m_tile[:, cs], True)
            value_block(vt_last, slot, cs, a_last)

    @pl.when((n_tiles - 1) % 2 == 0)
    def _():
        drain(0)

    @pl.when((n_tiles - 1) % 2 == 1)
    def _():
        drain(1)

    gt = gt_ref[0]
    tiles = []
    for jq in range(nsub):
        heads = []
        for hh in range(HEADS_PER_GROUP):
            cs = lanes(jq, hh)
            g_c = gt[hh:hh + 1, toks(jq)]
            g_s = gt[HEADS_PER_GROUP + hh:HEADS_PER_GROUP + hh + 1, toks(jq)]
            g_w = gt[2 * HEADS_PER_GROUP + hh:2 * HEADS_PER_GROUP + hh + 1, toks(jq)]
            heads.append(o_sc[0, 0:HEAD_DIM, cs] * g_c
                         + acc_sc[0:HEAD_DIM, cs] * (g_s / acc_sc[HEAD_DIM:HEAD_DIM + 1, cs])
                         + o_sc[1, 0:HEAD_DIM, cs] * (g_w / o_sc[1, HEAD_DIM:HEAD_DIM + 1, cs]))
        tiles.append(jnp.concatenate(heads, axis=0).T)
    y_ref[0] = jnp.concatenate(tiles, axis=0).astype(BF16)


def _nsa(qt, kcmp, vcmpt, ov, ks, vst, kw, vwt, gt, tq, tk, topn):
    B, _, _, S = qt.shape
    nc = kcmp.shape[1]
    cols = HEADS_PER_GROUP * tq
    kern = functools.partial(_nsa_kernel, tq=tq, tk=tk, topn=topn)
    seq_rows = pl.BlockSpec((1, 1, S, LANES), lambda b, g, i: (b, g, 0, 0))
    seq_cols = pl.BlockSpec((1, 1, LANES, S), lambda b, g, i: (b, g, 0, 0))
    return pl.pallas_call(
        kern,
        grid=(B, KV_GROUPS, S // tq),
        in_specs=[pl.BlockSpec((1, HEADS_PER_GROUP // 2, LANES, tq), lambda b, g, i: (b, g, 0, i)),
                  pl.BlockSpec((1, nc, LANES), lambda b, g, i: (b * KV_GROUPS + g, 0, 0)),
                  pl.BlockSpec((1, LANES, nc), lambda b, g, i: (b * KV_GROUPS + g, 0, 0)),
                  pl.BlockSpec((SEL_SLOTS, nc), lambda b, g, i: (0, 0)),
                  seq_rows, seq_cols, seq_rows, seq_cols,
                  pl.BlockSpec((1, GATE_ROWS, tq), lambda b, g, i: (b, g, i))],
        out_specs=pl.BlockSpec((1, tq, HEADS_PER_GROUP * HEAD_DIM), lambda b, g, i: (b, i, g)),
        out_shape=jax.ShapeDtypeStruct((B, S, N_HEADS * HEAD_DIM), BF16),
        scratch_shapes=[pltpu.VMEM((LANES, cols), BF16), pltpu.VMEM((LANES, cols), F32),
                        pltpu.VMEM((2, LANES, cols), F32),
                        pltpu.VMEM((tq // SUB_Q, WINDOW + SUB_Q, HEADS_PER_GROUP * SUB_Q), F32),
                        pltpu.VMEM((2, tk, cols), F32), pltpu.VMEM((2, tk, cols), BF16)],
        compiler_params=pltpu.CompilerParams(vmem_limit_bytes=VMEM_LIMIT),
    )(qt, kcmp, vcmpt, ov, ks, vst, kw, vwt, gt)


def _out_ffn_kernel(x_ref, mod_ref, n2_ref, yp_ref, yn_ref, wo_ref, w1_ref, w2_ref, o_ref, *, ff_chunk):
    x = x_ref[0]
    gate1 = mod_ref[0, 0, 2:3, :]
    shift2 = mod_ref[0, 0, 3:4, :]
    scale2 = mod_ref[0, 0, 4:5, :]
    gate2 = mod_ref[0, 0, 5:6, :]
    half = wo_ref.shape[0] // 2
    mix = _dot(yp_ref[0], wo_ref[0:half, :]) + _dot(yn_ref[0], wo_ref[half:2 * half, :])
    x1 = x + gate1 * mix
    ms = jnp.mean(x1 * x1, axis=-1, keepdims=True)
    h2 = ((x1 * lax.rsqrt(ms + EPS) * n2_ref[...]) * (1.0 + scale2) + shift2).astype(BF16)
    d_ff = w1_ref.shape[1]
    ff = jnp.zeros(x.shape, F32)
    for c in range(d_ff // ff_chunk):
        a = jnp.maximum(_dot(h2, w1_ref[:, c * ff_chunk:(c + 1) * ff_chunk]), 0.0)
        ff = ff + _dot((a * a).astype(BF16), w2_ref[c * ff_chunk:(c + 1) * ff_chunk, :])
    o_ref[0] = x1 + gate2 * ff


def _out_ffn(x, mod, layer, n2, yp, yn, wo, w1, w2, tm):
    B, S, D = x.shape
    kern = functools.partial(_out_ffn_kernel, ff_chunk=1024)
    tok = lambda width: pl.BlockSpec((1, tm, width), lambda b, i: (b, i, 0))
    per_layer = lambda a: pl.BlockSpec((None,) + a.shape[1:], lambda b, i: (layer,) + (0,) * (a.ndim - 1),
                                       pipeline_mode=pl.Buffered(1))
    return pl.pallas_call(
        kern,
        grid=(B, S // tm),
        in_specs=[tok(D),
                  pl.BlockSpec((1, 1, 6, D), lambda b, i: (layer, b, 0, 0)),
                  per_layer(n2),
                  tok(512), tok(512),
                  per_layer(wo), per_layer(w1), per_layer(w2)],
        out_specs=tok(D),
        out_shape=jax.ShapeDtypeStruct((B, S, D), F32),
        compiler_params=pltpu.CompilerParams(vmem_limit_bytes=VMEM_LIMIT),
    )(x, mod, n2, yp, yn, wo, w1, w2)


def _rot_cols(w):
    shp = w.shape
    w4 = w.reshape(shp[:-1] + (shp[-1] // HEAD_DIM, 2, HEAD_DIM // 2))
    return jnp.stack([-w4[..., 1, :], w4[..., 0, :]], axis=-2).reshape(shp)


def _swap_halves(g):
    half = HEAD_DIM // 2
    return jnp.concatenate([g[..., half:], g[..., :half]], axis=-1)


def _rope_tables(pos, reps, signed):
    half = HEAD_DIM // 2
    inv = ROPE_THETA ** (-jnp.arange(half, dtype=F32) * 2.0 / HEAD_DIM)
    ang = pos.astype(F32)[:, None] * inv[None, :]
    cos = jnp.tile(jnp.cos(ang), (1, 2 * reps))
    sin = jnp.sin(ang)
    sin = jnp.tile(jnp.concatenate([-sin if signed else sin, sin], axis=1), (1, reps))
    return cos, sin


def _overlap(n_chunks, n_sel):
    cs0 = np.arange(n_chunks) * CMP_STRIDE
    ss0 = np.arange(n_sel) * SEL_LEN
    ov = np.minimum(cs0[:, None] + CMP_LEN, ss0[None, :] + SEL_LEN) - np.maximum(cs0[:, None], ss0[None, :])
    ov = np.clip(ov, 0, None).astype(np.float32) / CMP_LEN
    out = np.zeros((SEL_SLOTS, n_chunks), np.float32)
    out[:n_sel] = ov.T
    return out


def _gate_cols(w_g):
    idx = np.zeros((LANES,), np.int32)
    used = np.zeros((LANES,), np.float32)
    for g in range(KV_GROUPS):
        for br in range(N_BRANCH):
            for hg in range(HEADS_PER_GROUP):
                c = g * GATE_ROWS + br * HEADS_PER_GROUP + hg
                idx[c] = (g * HEADS_PER_GROUP + hg) * N_BRANCH + br
                used[c] = 1.0
    return w_g[..., idx] * used


def kernel(x, c, w_mod, b_mod, norm1, norm2, w_in, pool_w, pool_scale, q_norm, k_norm, cmp_pe, cmp_w1, cmp_w2, w_out, w_ff1, w_ff2):
    B, S, D = x.shape
    depth = w_mod.shape[0]
    n_chunks = S // CMP_STRIDE
    n_sel = S // SEL_LEN
    assert D == 1024 and S % 2048 == 0 and n_sel <= SEL_SLOTS
    topn = min(SEL_TOPN, n_sel)
    tm_in, tm_out, tq, tk = 512, 512, 512, 512

    mod = _modulation(c, w_mod, b_mod)

    w_b = w_in.astype(BF16)
    w_g = _gate_cols(w_in[:, :, C_END:]).astype(BF16)
    gq = jnp.tile(q_norm, (1, N_HEADS))[:, None, :]
    gk = jnp.stack([jnp.tile(k_norm[:, 1], (1, 2)), jnp.tile(k_norm[:, 2], (1, 2))], axis=1)
    pad64 = ((0, 0), (0, HEAD_DIM))
    kn = jnp.stack([jnp.pad(k_norm[:, 0], pad64), jnp.pad(_swap_halves(k_norm[:, 0]), pad64)], axis=1)
    w1_k = jnp.pad(cmp_w1[:, 0].astype(BF16), ((0, 0), (0, 0), (0, HEAD_DIM)))
    w1_v = jnp.pad(cmp_w1[:, 1].astype(BF16), ((0, 0), (0, 0), (HEAD_DIM, 0)))
    w1c = jnp.stack([w1_k.reshape(depth, CMP_LEN, HEAD_DIM, LANES),
                     w1_v.reshape(depth, CMP_LEN, HEAD_DIM, LANES)],
                    axis=2).reshape(depth, CMP_LEN, LANES, LANES)
    zero_w2 = jnp.zeros_like(cmp_w2[:, 0])
    w2c = jnp.concatenate([
        jnp.concatenate([cmp_w2[:, 0], zero_w2, _rot_cols(cmp_w2[:, 0]), zero_w2, zero_w2, zero_w2], axis=-1),
        jnp.concatenate([zero_w2, zero_w2, zero_w2, zero_w2, cmp_w2[:, 1], zero_w2], axis=-1)],
        axis=-2).astype(BF16)
    pe = jnp.concatenate([cmp_pe[:, 0], cmp_pe[:, 1]], axis=-1)
    pool_wb = pool_w.astype(BF16)
    wo_b, w1_b, w2_b = w_out.astype(BF16), w_ff1.astype(BF16), w_ff2.astype(BF16)

    cos, sin = _rope_tables(jnp.arange(S), 2, signed=True)
    cos_c, sin_c = _rope_tables(jnp.arange(n_chunks) * CMP_STRIDE + (CMP_LEN - 1), 2, signed=False)
    bd = jnp.asarray(np.kron(np.eye(N_HEADS, dtype=np.float32),
                             np.full((HEAD_DIM, HEAD_DIM), 1.0 / HEAD_DIM, np.float32)), BF16)
    ov = jnp.asarray(_overlap(n_chunks, n_sel), BF16)

    for l in range(depth):
        y_pool, qt, ks, kw, vst, vwt, kvc, gt = _inproj(
            x, mod, l, norm1[:, None, :], w_b, w_g, cos, sin, gq, gk, bd,
            pool_wb, pool_scale[:, None, :], tm_in)
        kcmp, vcmpt = _compress(kvc, l, pe, w1c, w2c, kn, cos_c, sin_c)
        y_nsa = _nsa(qt, kcmp, vcmpt, ov, ks, vst, kw, vwt, gt, tq, tk, topn)
        x = _out_ffn(x, mod, l, norm2[:, None, :], y_pool, y_nsa, wo_b, w1_b, w2_b, tm_out)
    return x
```

```python
import functools
import math

import numpy as np
import jax
import jax.numpy as jnp
from jax import lax
from jax.experimental import pallas as pl
from jax.experimental.pallas import tpu as pltpu

F32 = jnp.float32
BF16 = jnp.bfloat16

LANES = 128
SUBLANES = 8
HEAD_DIM = 64
N_HEADS = 8
KV_GROUPS = 2
HEADS_PER_GROUP = N_HEADS // KV_GROUPS
N_BRANCH = 3
GATE_ROWS = 16
POOL_WINDOWS = (2, 4, 8, 16)
POOL_HALO = 16
CMP_LEN = 32
CMP_STRIDE = 16
SEL_LEN = 64
SEL_SLOTS = 64
SEL_TOPN = 16
SUB_Q = 128
WINDOW = 512
ROPE_THETA = 10000.0
EPS = 1e-6
NEG = -1e30
FORCED_SCORE = 1e4
LOG2E = math.log2(math.e)
SCALE = 1.0 / math.sqrt(HEAD_DIM)
VMEM_LIMIT = 56 * 1024 * 1024
MXU_COLS = 256


def _dot(a, b):
    return jnp.dot(a, b, preferred_element_type=F32)


def _split_bf16(v):
    hi = v.astype(BF16)
    lo = (v - hi.astype(F32)).astype(BF16)
    return hi, lo


def _mod_kernel(c_ref, w_ref, b_ref, o_ref, *, batch):
    c = c_ref[...]
    rows = c.shape[0]
    ca = c * jax.nn.sigmoid(c)
    hi = ca.astype(BF16).astype(F32)
    row = lax.broadcasted_iota(jnp.int32, ca.shape, 0)
    a = jnp.where(row < batch, hi, ca - hi).astype(BF16)
    w = w_ref[0]
    w_hi, w_lo = _split_bf16(w)
    r1 = _dot(a, w_hi)
    r2 = _dot(a, w_lo)
    o_ref[0] = r1 + pltpu.roll(r1, rows - batch, axis=0) + r2 + b_ref[0]


def _modulation(c, w_mod, b_mod):
    depth, d, n = w_mod.shape
    batch = c.shape[0]
    rows = -(-2 * batch // SUBLANES) * SUBLANES
    c2 = jnp.concatenate([c, c, jnp.zeros((rows - 2 * batch, d), F32)], axis=0)
    tn = 1536
    out = pl.pallas_call(
        functools.partial(_mod_kernel, batch=batch),
        grid=(depth, n // tn),
        in_specs=[pl.BlockSpec((rows, d), lambda l, j: (0, 0)),
                  pl.BlockSpec((1, d, tn), lambda l, j: (l, 0, j)),
                  pl.BlockSpec((1, 1, tn), lambda l, j: (l, 0, j))],
        out_specs=pl.BlockSpec((1, rows, tn), lambda l, j: (l, 0, j)),
        out_shape=jax.ShapeDtypeStruct((depth, rows, n), F32),
        compiler_params=pltpu.CompilerParams(vmem_limit_bytes=VMEM_LIMIT),
    )(c2, w_mod, b_mod.reshape(depth, 1, n))
    return out[:, :batch].reshape(depth, batch, 6, d)


C_U = 0
C_Q = 512
C_KV = 1024
C_END = 1792


def _inproj_kernel(x_ref, mod_ref, n1_ref, w_ref, wg_ref, cos_ref, sin_ref, gq_ref, gk_ref, bd_ref,
                   pw_ref, psc_ref,
                   yp_ref, qt_ref, kk_ref, vst_ref, vwt_ref, kvc_ref, gt_ref, ext_sc, *, tm):
    i = pl.program_id(1)
    shift = mod_ref[0, 0, 0:1, :]
    scale = mod_ref[0, 0, 1:2, :]

    x = x_ref[0]
    ms = jnp.mean(x * x, axis=-1, keepdims=True)
    hb = ((x * lax.rsqrt(ms + EPS) * n1_ref[...]) * (1.0 + scale) + shift).astype(BF16)

    def proj(a, b):
        return _dot(hb, w_ref[:, a:b])

    def seg_mean(v, bdm):
        return _dot((v * v).astype(BF16), bdm)

    def rope(y, cos_t, sin_t):
        width = y.shape[1]
        first = (lax.broadcasted_iota(jnp.int32, y.shape, 1) & (HEAD_DIM // 2)) == 0
        rot = jnp.where(first, pltpu.roll(y, width - HEAD_DIM // 2, axis=1),
                        pltpu.roll(y, HEAD_DIM // 2, axis=1))
        return y * cos_t + rot * sin_t

    @pl.when(i == 0)
    def _():
        ext_sc[0:POOL_HALO, :] = jnp.zeros((POOL_HALO, C_Q), F32)

    @pl.when(i > 0)
    def _():
        ext_sc[0:POOL_HALO, :] = ext_sc[tm:tm + POOL_HALO, :]

    ext_sc[POOL_HALO:POOL_HALO + tm, :] = proj(C_U, C_Q)
    q = proj(C_Q, C_KV)
    kv = proj(C_KV, C_END)
    gates = _dot(hb, wg_ref[...])
    pos = i * tm + lax.broadcasted_iota(jnp.int32, (tm, LANES), 0)
    for g, w in enumerate(POOL_WINDOWS):
        c0, c1 = g * LANES, (g + 1) * LANES
        cur = ext_sc[POOL_HALO:POOL_HALO + tm, c0:c1]
        acc = cur
        for k in range(1, w):
            acc = acc + ext_sc[POOL_HALO - k:POOL_HALO - k + tm, c0:c1]
        cnt = jnp.minimum(pos + 1, w).astype(F32)
        pooled = acc / cnt - cur
        yp_ref[0, :, c0:c1] = (_dot(pooled.astype(BF16), pw_ref[g]) * psc_ref[:, c0:c1]).astype(BF16)

    cos = cos_ref[...]
    sin = sin_ref[...]
    cos4 = jnp.concatenate([cos] * 4, axis=1)
    sin4 = jnp.concatenate([sin] * 4, axis=1)
    rq = lax.rsqrt(seg_mean(q, bd_ref[...]) + EPS)
    qo = rope(q * gq_ref[...], cos4, sin4) * (rq * (SCALE * LOG2E))
    for hp in range(N_HEADS // 2):
        qt_ref[0, hp] = qo[:, LANES * hp:LANES * (hp + 1)].T.astype(BF16)

    bd2 =bd_ref[0:LANES, 0:LANES]
    k_c, k_s, k_w = kv[:, 0:128], kv[:, 128:256], kv[:, 256:384]
    v_c, v_s, v_w = kv[:, 384:512], kv[:, 512:640], kv[:, 640:768]
    kso = rope(k_s * gk_ref[0:1, :], cos, sin) * lax.rsqrt(seg_mean(k_s, bd2) + EPS)
    kwo = rope(k_w * gk_ref[1:2, :], cos, sin) * lax.rsqrt(seg_mean(k_w, bd2) + EPS)

    lane = lax.broadcasted_iota(jnp.int32, (tm, LANES), 1)
    low = lane < HEAD_DIM
    pos = i * tm + lax.broadcasted_iota(jnp.int32, (tm, LANES), 0)
    blk = pos // SEL_LEN
    blk_flag = jnp.where(lane - HEAD_DIM == blk, -NEG, 0.0)
    ones_col = jnp.where(lane == HEAD_DIM, 1.0, 0.0)
    for g in range(KV_GROUPS):
        def grp(a, g=g):
            return pltpu.roll(a, HEAD_DIM, axis=1) if g else a
        kk_ref[0, g, 0] = jnp.where(low, grp(kwo), 0.0).astype(BF16)
        kk_ref[0, g, 1] = jnp.where(low, grp(kso), blk_flag).astype(BF16)
        vst_ref[0, g] = jnp.where(low, grp(v_s), ones_col).T.astype(BF16)
        vwt_ref[0, g] = jnp.where(low, grp(v_w), ones_col).T.astype(BF16)
        kvc_ref[0, g] = jnp.where(low, grp(k_c), v_c if g else pltpu.roll(v_c, HEAD_DIM, axis=1))
    gt_ref[0] = jax.nn.sigmoid(gates).T[0:KV_GROUPS * GATE_ROWS]


def _inproj(x, mod, layer, n1, w, wg, cos, sin, gq, gk, bd, pw, psc, tm):
    B, S, D = x.shape
    kern = functools.partial(_inproj_kernel, tm=tm)
    grp_rows = lambda width: pl.BlockSpec((1, KV_GROUPS, tm, width), lambda b, i: (b, 0, i, 0))
    grp_cols = pl.BlockSpec((1, KV_GROUPS, LANES, tm), lambda b, i: (b, 0, 0, i))
    rows_shape = lambda width: jax.ShapeDtypeStruct((B, KV_GROUPS, S, width), BF16)
    cols_shape = jax.ShapeDtypeStruct((B, KV_GROUPS, LANES, S), BF16)
    const = lambda shape: pl.BlockSpec(shape, lambda b, i: (0,) * len(shape))
    per_layer = lambda a: pl.BlockSpec((None,) + a.shape[1:], lambda b, i: (layer,) + (0,) * (a.ndim - 1))
    return pl.pallas_call(
        kern,
        grid=(B, S // tm),
        in_specs=[pl.BlockSpec((1, tm, D), lambda b, i: (b, i, 0)),
                  pl.BlockSpec((1, 1, 6, D), lambda b, i: (layer, b, 0, 0)),
                  per_layer(n1), per_layer(w), per_layer(wg),
                  pl.BlockSpec((tm, LANES), lambda b, i: (i, 0)),
                  pl.BlockSpec((tm, LANES), lambda b, i: (i, 0)),
                  per_layer(gq), per_layer(gk), const(bd.shape),
                  per_layer(pw), per_layer(psc)],
        out_specs=[pl.BlockSpec((1, tm, C_Q), lambda b, i: (b, i, 0)),
                   pl.BlockSpec((1, N_HEADS // 2, LANES, tm), lambda b, i: (b, 0, 0, i)),
                   pl.BlockSpec((1, KV_GROUPS, 2, tm, LANES), lambda b, i: (b, 0, 0, i, 0)),
                   grp_cols, grp_cols,
                   grp_rows(LANES),
                   pl.BlockSpec((1, KV_GROUPS * GATE_ROWS, tm), lambda b, i: (b, 0, i))],
        out_shape=[jax.ShapeDtypeStruct((B, S, C_Q), BF16),
                   jax.ShapeDtypeStruct((B, N_HEADS // 2, LANES, S), BF16),
                   jax.ShapeDtypeStruct((B, KV_GROUPS, 2, S, LANES), BF16), cols_shape, cols_shape,
                   jax.ShapeDtypeStruct((B, KV_GROUPS, S, LANES), F32),
                   jax.ShapeDtypeStruct((B, KV_GROUPS * GATE_ROWS, S), F32)],
        scratch_shapes=[pltpu.VMEM((tm + POOL_HALO, C_Q), F32)],
        compiler_params=pltpu.CompilerParams(vmem_limit_bytes=VMEM_LIMIT),
    )(x, mod, n1, w, wg, cos, sin, gq, gk, bd, pw, psc)


def _compress_kernel(x_ref, pe_ref, w1_ref, w2_ref, kn_ref, cos_ref, sin_ref, ko_ref, vot_ref):
    nc = ko_ref.shape[1]
    taps = CMP_LEN // 2
    first = jnp.zeros((nc, LANES), F32)
    second = jnp.zeros((nc, LANES), F32)
    for t in range(taps):
        x = x_ref[0, 0, pl.ds(t, nc, stride=CMP_STRIDE), :]
        first = first + _dot((x + pe_ref[t:t + 1, :]).astype(BF16), w1_ref[t])
        second = second + _dot((x + pe_ref[taps + t:taps + t + 1, :]).astype(BF16), w1_ref[taps + t])
    pre = first + pltpu.roll(second, nc - 1, axis=0)
    h = (pre * jax.nn.sigmoid(pre)).astype(BF16)
    kkv = _dot(h, w2_ref[...])
    k = kkv[:, 0:LANES]
    kr = kkv[:, LANES:2 * LANES]
    r = lax.rsqrt(jnp.sum(k * k, axis=-1, keepdims=True) * (1.0 / HEAD_DIM) + EPS)
    ko = (k * kn_ref[0:1, :] * cos_ref[...] + kr * kn_ref[1:2, :] * sin_ref[...]) * r
    ko_ref[0] = ko.astype(BF16)
    vot_ref[0] = kkv[:, 2 * LANES:3 * LANES].T.astype(BF16)


def _compress(kvc, layer, pe, w1, w2, kn, cos_c, sin_c):
    B, G, S, _ = kvc.shape
    nc = S // CMP_STRIDE
    const = lambda shape: pl.BlockSpec(shape, lambda b, g: (0,) * len(shape))
    per_layer = lambda a: pl.BlockSpec((None,) + a.shape[1:], lambda b, g: (layer,) + (0,) * (a.ndim - 1))
    return pl.pallas_call(
        _compress_kernel,
        grid=(B, G),
        in_specs=[pl.BlockSpec((1, 1, S, LANES), lambda b, g: (b, g, 0, 0)),
                  per_layer(pe), per_layer(w1), per_layer(w2), per_layer(kn),
                  const(cos_c.shape), const(sin_c.shape)],
        out_specs=[pl.BlockSpec((1, nc, LANES), lambda b, g: (b * G + g, 0, 0)),
                   pl.BlockSpec((1, LANES, nc), lambda b, g: (b * G + g, 0, 0))],
        out_shape=[jax.ShapeDtypeStruct((B * G, nc, LANES), BF16),
                   jax.ShapeDtypeStruct((B * G, LANES, nc), BF16)],
        compiler_params=pltpu.CompilerParams(vmem_limit_bytes=VMEM_LIMIT),
    )(kvc, pe, w1, w2, kn, cos_c, sin_c)


def _nsa_kernel(qt_ref, kc_ref, vct_ref, ov_ref, k_ref, vst_ref, vwt_ref, gt_ref,
                y_ref, q_sc, acc_sc, o_sc, w_sc, s_sc, p_sc, *, tq, tk, topn):
    i = pl.program_id(2)
    seq = k_ref.shape[2] // 2
    nsub = tq // SUB_Q
    wsub = HEADS_PER_GROUP * SUB_Q
    cols = nsub * wsub
    nc = kc_ref.shape[1]

    def lanes(jq, hh):
        return slice(jq * wsub + hh * SUB_Q, jq * wsub + (hh + 1) * SUB_Q)

    def toks(jq):
        return slice(jq * SUB_Q, (jq + 1) * SUB_Q)

    for jq in range(nsub):
        for hh in range(HEADS_PER_GROUP):
            q_sc[0:HEAD_DIM, lanes(jq, hh)] = qt_ref[
                0, hh // 2, (hh % 2) * HEAD_DIM:(hh % 2 + 1) * HEAD_DIM, toks(jq)]
    q_sc[HEAD_DIM:LANES, :] = jnp.full((LANES - HEAD_DIM, cols), -1.0, BF16)

    def tok(shape, lane0=0):
        lane = lane0 + lax.broadcasted_iota(jnp.int32, shape, 1)
        return i * tq + (lane // wsub) * SUB_Q + (lane & (SUB_Q - 1))

    nwin = WINDOW // SUB_Q
    kk = lax.broadcasted_iota(jnp.int32, (SUB_Q, wsub), 0)
    r = lax.broadcasted_iota(jnp.int32, (SUB_Q, wsub), 1) & (SUB_Q - 1)

    def chunk(jq, c):
        return i * nsub + jq - nwin + c

    def window_scores(jq):
        q_sub = q_sc[:, jq * wsub:(jq + 1) * wsub]
        for c in range(nwin + 1):
            ci = chunk(jq, c)
            start = pl.multiple_of(jnp.where(ci >= 0, ci * SUB_Q, seq), SUB_Q)
            sw = _dot(k_ref[0, 0, pl.ds(start, SUB_Q), :], q_sub)
            if c == nwin:
                sw = jnp.where(kk <= r, sw, NEG)
            elif c == 0:
                sw = jnp.where(kk > r, sw, NEG)
            w_sc[jq, c * SUB_Q:(c + 1) * SUB_Q, :] = sw

    def window_finish(jq):
        sw = w_sc[jq]
        pw = jnp.exp2(sw - jnp.max(sw, axis=0, keepdims=True)).astype(BF16)
        ow = jnp.zeros((LANES, wsub), F32)
        for c in range(nwin + 1):
            start = pl.multiple_of(jnp.maximum(chunk(jq, c), 0) * SUB_Q, SUB_Q)
            ow = ow + _dot(vwt_ref[0, 0, :, pl.ds(start, SUB_Q)], pw[c * SUB_Q:(c + 1) * SUB_Q])
        o_sc[1, :, jq * wsub:(jq + 1) * wsub] = ow

    s = _dot(kc_ref[0], q_sc[...])
    for jq in range(nsub):
        window_scores(jq)
    last_block = lax.shift_right_arithmetic(tok((1, cols)) - (CMP_LEN - 1), CMP_STRIDE.bit_length() - 1)
    s = jnp.where(lax.broadcasted_iota(jnp.int32, (nc, cols), 0) <= last_block, s, NEG)
    m = jnp.max(s, axis=0, keepdims=True)
    p = jnp.exp2(s - jnp.maximum(m, 0.5 * NEG))
    l = jnp.sum(p, axis=0, keepdims=True)
    inv_l = 1.0 / jnp.where(l > 0.0, l, 1.0)
    o_sc[0] = _dot(vct_ref[0], p.astype(BF16)) * inv_l

    sums = []
    for jq in range(nsub):
        ps = p[:, lanes(jq, 0)] * inv_l[:, lanes(jq, 0)]
        for hh in range(1, HEADS_PER_GROUP):
            ps = ps + p[:, lanes(jq, hh)] * inv_l[:, lanes(jq, hh)]
        sums.append(ps)
    hi, lo = _split_bf16(jnp.concatenate(sums, axis=1))
    ov = ov_ref[...]
    imp = _dot(ov, hi) + _dot(ov, lo)

    for jq in range(nsub - 1):
        window_finish(jq)

    j = lax.broadcasted_iota(jnp.int32, (SEL_SLOTS, tq), 0)
    cur = (i * tq + lax.broadcasted_iota(jnp.int32, (SEL_SLOTS, tq), 1)) // SEL_LEN
    forced = (j == 0) | (j == cur) | (j == cur - 1)
    imp = jnp.where(j <= cur, jnp.where(forced, FORCED_SCORE, imp), -1.0)
    def ranked(n_rows):
        def run():
            sub = SUBLANES
            slabs = [imp[a:a + sub] for a in range(0, n_rows, sub)]
            ranks = [jnp.zeros((sub, tq), F32) for _ in slabs]
            sub_row = lax.broadcasted_iota(jnp.int32, (sub, tq), 0)
            for jj in range(n_rows):
                rowj = jnp.broadcast_to(imp[jj:jj + 1, :], (sub, tq))
                for a, xs in enumerate(slabs):
                    if a * sub > jj:
                        inc = jnp.where(rowj >= xs, 1.0, 0.0)
                    elif a * sub + sub - 1 <= jj:
                        inc = jnp.where(rowj > xs, 1.0, 0.0)
                    else:
                        tie = jnp.where(sub_row > jj - a * sub, 1.0, 0.0)
                        inc = jnp.where(rowj > xs, 1.0, jnp.where(rowj == xs, tie, 0.0))
                    ranks[a] = ranks[a] + inc
            rank = jnp.concatenate(ranks, axis=0)
            kept = jnp.where((rank < float(topn)) & (imp[0:n_rows] >= 0.0), 0.0, -1.0)
            if n_rows == SEL_SLOTS:
                return kept
            return jnp.concatenate([kept, jnp.full((SEL_SLOTS - n_rows, tq), -1.0, F32)], axis=0)
        return run

    row_options = tuple(r for r in (SEL_SLOTS // 2, 3 * SEL_SLOTS // 4) if r > topn) + (SEL_SLOTS,)
    visible = (i + 1) * tq // SEL_LEN
    branch = sum((visible > r).astype(jnp.int32) for r in (topn,) + row_options[:-1])
    flag = lax.switch(branch, [lambda: jnp.where(imp >= 0.0, 0.0, -1.0)]
                      + [ranked(r) for r in row_options]).astype(BF16)

    for jq in range(nsub):
        for hh in range(HEADS_PER_GROUP):
            q_sc[HEAD_DIM:LANES, lanes(jq, hh)] = flag[:, toks(jq)]
    acc_sc[...] = jnp.zeros(acc_sc.shape, F32)

    blocks = [slice(a, a + MXU_COLS) for a in range(0, cols, MXU_COLS)]

    def key_tile(kt):
        return k_ref[0, 0, pl.ds(pl.multiple_of(seq + kt * tk, tk), tk), :]

    def value_tile(kt):
        return vst_ref[0, 0, :, pl.ds(pl.multiple_of(jnp.maximum(kt, 0) * tk, tk), tk)]

    def score_block(k, slot, cs):
        ss = _dot(k, q_sc[:, cs])
        s_sc[slot, :, cs] = ss
        return jnp.max(ss, axis=0, keepdims=True)

    def softmax_block(slot, cs, m_prev, m_blk, diagonal):
        ss = s_sc[slot, :, cs]
        if diagonal:
            m_blk = jnp.max(ss, axis=0, keepdims=True)
        m_new = jnp.maximum(m_prev, m_blk)
        p_sc[slot, :, cs] = jnp.exp2(ss - m_new).astype(BF16)
        return m_new, jnp.exp2(m_prev - m_new)

    def value_block(vt, slot, cs, a_blk):
        acc_sc[:, cs] = acc_sc[:, cs] * a_blk + _dot(vt, p_sc[slot, :, cs])

    def row(parts):
        return jnp.concatenate(parts, axis=1)

    n_tiles = (i * tq) // tk + 1
    k_first = key_tile(0)
    m_first = row([score_block(k_first, 0, cs) for cs in blocks])
    p_sc[1] = jnp.zeros(p_sc.shape[1:], BF16)
    window_finish(nsub - 1)

    def steady(k, carry):
        m_run, alpha, m_tile = carry

        def arm(slot):
            kx, vt = key_tile(k), value_tile(k - 2)
            m_new, a_new, m_next = [], [], []
            for cs in blocks:
                mn, an = softmax_block(1 - slot, cs, m_run[:, cs], m_tile[:, cs], False)
                m_next.append(score_block(kx, slot, cs))
                value_block(vt, slot, cs, alpha[:, cs])
                m_new.append(mn)
                a_new.append(an)
            return row(m_new), row(a_new), row(m_next)

        return lax.cond(k % 2 == 0, lambda: arm(0), lambda: arm(1))

    m_run, alpha, m_tile = lax.fori_loop(
        1, n_tiles, steady,
        (jnp.full((1, cols), NEG, F32), jnp.ones((1, cols), F32), m_first))

    def drain(slot):
        row0 = i * tq - (n_tiles - 1) * tk
        vt_prev, vt_last = value_tile(n_tiles - 2), value_tile(n_tiles - 1)
        for cs in blocks:
            value_block(vt_prev, 1 - slot, cs, alpha[:, cs])
        for jq in range(nsub):
            rows = pl.ds(pl.multiple_of(row0 + jq * SUB_Q, SUB_Q), SUB_Q)
            own = slice(jq * wsub, (jq + 1) * wsub)
            s_sc[slot, rows, own] = jnp.where(kk <= r, s_sc[slot, rows, own], NEG)
        for cs in blocks:
            _, a_last = softmax_block(slot, cs, m_run[:, cs], m_tile[:, cs], True)
            value_block(vt_last, slot, cs, a_last)

    @pl.when((n_tiles - 1) % 2 == 0)
    def _():
        drain(0)

    @pl.when((n_tiles - 1) % 2 == 1)
    def _():
        drain(1)

    gt = gt_ref[0]
    tiles = []
    for jq in range(nsub):
        heads = []
        for hh in range(HEADS_PER_GROUP):
            cs = lanes(jq, hh)
            g_c = gt[hh:hh + 1, toks(jq)]
            g_s = gt[HEADS_PER_GROUP + hh:HEADS_PER_GROUP + hh + 1, toks(jq)]
            g_w = gt[2 * HEADS_PER_GROUP + hh:2 * HEADS_PER_GROUP + hh + 1, toks(jq)]
            heads.append(o_sc[0, 0:HEAD_DIM, cs] * g_c
                         + acc_sc[0:HEAD_DIM, cs] * (g_s / acc_sc[HEAD_DIM:HEAD_DIM + 1, cs])
                         + o_sc[1, 0:HEAD_DIM, cs] * (g_w / o_sc[1, HEAD_DIM:HEAD_DIM + 1, cs]))
        tiles.append(jnp.concatenate(heads, axis=0).T)
    y_ref[0] = jnp.concatenate(tiles, axis=0).astype(BF16)


def _nsa(qt, kcmp, vcmpt, ov, keys, vst, vwt, gt, tq, tk, topn):
    B, _, _, S = qt.shape
    nc = kcmp.shape[1]
    cols = HEADS_PER_GROUP * tq
    kern = functools.partial(_nsa_kernel, tq=tq, tk=tk, topn=topn)
    seq_rows = pl.BlockSpec((1, 1, 2 * S, LANES), lambda b, g, i: (b, g, 0, 0))
    seq_cols = pl.BlockSpec((1, 1, LANES, S), lambda b, g, i: (b, g, 0, 0))
    return pl.pallas_call(
        kern,
        grid=(B, KV_GROUPS, S // tq),
        in_specs=[pl.BlockSpec((1, HEADS_PER_GROUP // 2, LANES, tq), lambda b, g, i: (b, g, 0, i)),
                  pl.BlockSpec((1, nc, LANES), lambda b, g, i: (b * KV_GROUPS + g, 0, 0)),
                  pl.BlockSpec((1, LANES, nc), lambda b, g, i: (b * KV_GROUPS + g, 0, 0)),
                  pl.BlockSpec((SEL_SLOTS, nc), lambda b, g, i: (0, 0)),
                  seq_rows, seq_cols, seq_cols,
                  pl.BlockSpec((1, GATE_ROWS, tq), lambda b, g, i: (b, g, i))],
        out_specs=pl.BlockSpec((1, tq, HEADS_PER_GROUP * HEAD_DIM), lambda b, g, i: (b, i, g)),
        out_shape=jax.ShapeDtypeStruct((B, S, N_HEADS * HEAD_DIM), BF16),
        scratch_shapes=[pltpu.VMEM((LANES, cols), BF16), pltpu.VMEM((LANES, cols), F32),
                        pltpu.VMEM((2, LANES, cols), F32),
                        pltpu.VMEM((tq // SUB_Q, WINDOW + SUB_Q, HEADS_PER_GROUP * SUB_Q), F32),
                        pltpu.VMEM((2, tk, cols), F32), pltpu.VMEM((2, tk, cols), BF16)],
        compiler_params=pltpu.CompilerParams(vmem_limit_bytes=VMEM_LIMIT),
    )(qt, kcmp, vcmpt, ov, keys, vst, vwt, gt)


def _out_ffn_kernel(x_ref, mod_ref, n2_ref, yp_ref, yn_ref, wo_ref, w1_ref, w2_ref, o_ref, *, ff_chunk):
    x = x_ref[0]
    gate1 = mod_ref[0, 0, 2:3, :]
    shift2 = mod_ref[0, 0, 3:4, :]
    scale2 = mod_ref[0, 0, 4:5, :]
    gate2 = mod_ref[0, 0, 5:6, :]
    half = wo_ref.shape[0] // 2
    mix = _dot(yp_ref[0], wo_ref[0:half, :]) + _dot(yn_ref[0], wo_ref[half:2 * half, :])
    x1 = x + gate1 * mix
    ms = jnp.mean(x1 * x1, axis=-1, keepdims=True)
    h2 = ((x1 * lax.rsqrt(ms + EPS) * n2_ref[...]) * (1.0 + scale2) + shift2).astype(BF16)
    d_ff = w1_ref.shape[1]
    ff = jnp.zeros(x.shape, F32)
    for c in range(d_ff // ff_chunk):
        a = jnp.maximum(_dot(h2, w1_ref[:, c * ff_chunk:(c + 1) * ff_chunk]), 0.0)
        ff = ff + _dot((a * a).astype(BF16), w2_ref[c * ff_chunk:(c + 1) * ff_chunk, :])
    o_ref[0] = x1 + gate2 * ff


def _out_ffn(x, mod, layer, n2, yp, yn, wo, w1, w2, tm):
    B, S, D = x.shape
    kern = functools.partial(_out_ffn_kernel, ff_chunk=1024)
    tok = lambda width: pl.BlockSpec((1, tm, width), lambda b, i: (b, i, 0))
    per_layer = lambda a: pl.BlockSpec((None,) + a.shape[1:], lambda b, i: (layer,) + (0,) * (a.ndim - 1),
                                       pipeline_mode=pl.Buffered(1))
    return pl.pallas_call(
        kern,
        grid=(B, S // tm),
        in_specs=[tok(D),
                  pl.BlockSpec((1, 1, 6, D), lambda b, i: (layer, b, 0, 0)),
                  per_layer(n2),
                  tok(512), tok(512),
                  per_layer(wo), per_layer(w1), per_layer(w2)],
        out_specs=tok(D),
        out_shape=jax.ShapeDtypeStruct((B, S, D), F32),
        compiler_params=pltpu.CompilerParams(vmem_limit_bytes=VMEM_LIMIT),
    )(x, mod, n2, yp, yn, wo, w1, w2)


def _rot_cols(w):
    shp = w.shape
    w4 = w.reshape(shp[:-1] + (shp[-1] // HEAD_DIM, 2, HEAD_DIM // 2))
    return jnp.stack([-w4[..., 1, :], w4[..., 0, :]], axis=-2).reshape(shp)


def _swap_halves(g):
    half = HEAD_DIM // 2
    return jnp.concatenate([g[..., half:], g[..., :half]], axis=-1)


def _rope_tables(pos, reps, signed):
    half = HEAD_DIM // 2
    inv = ROPE_THETA ** (-jnp.arange(half, dtype=F32) * 2.0 / HEAD_DIM)
    ang = pos.astype(F32)[:, None] * inv[None, :]
    cos = jnp.tile(jnp.cos(ang), (1, 2 * reps))
    sin = jnp.sin(ang)
    sin = jnp.tile(jnp.concatenate([-sin if signed else sin, sin], axis=1), (1, reps))
    return cos, sin


def _overlap(n_chunks, n_sel):
    cs0 = np.arange(n_chunks) * CMP_STRIDE
    ss0 = np.arange(n_sel) * SEL_LEN
    ov = np.minimum(cs0[:, None] + CMP_LEN, ss0[None, :] + SEL_LEN) - np.maximum(cs0[:, None], ss0[None, :])
    ov = np.clip(ov, 0, None).astype(np.float32) / CMP_LEN
    out = np.zeros((SEL_SLOTS, n_chunks), np.float32)
    out[:n_sel] = ov.T
    return out


def _gate_cols(w_g):
    lead = w_g.shape[:-1]
    by_head = w_g.reshape(lead + (KV_GROUPS, HEADS_PER_GROUP, N_BRANCH))
    by_branch = jnp.swapaxes(by_head, -1, -2).reshape(lead + (KV_GROUPS, N_BRANCH * HEADS_PER_GROUP))
    pad_rows = GATE_ROWS - N_BRANCH * HEADS_PER_GROUP
    grouped = jnp.pad(by_branch, [(0, 0)] * len(lead) + [(0, 0), (0, pad_rows)])
    flat = grouped.reshape(lead + (KV_GROUPS * GATE_ROWS,))
    return jnp.pad(flat, [(0, 0)] * len(lead) + [(0, LANES - KV_GROUPS * GATE_ROWS)])


def kernel(x, c, w_mod, b_mod, norm1, norm2, w_in, pool_w, pool_scale, q_norm, k_norm, cmp_pe, cmp_w1, cmp_w2, w_out, w_ff1, w_ff2):
    B, S, D = x.shape
    depth = w_mod.shape[0]
    n_chunks = S // CMP_STRIDE
    n_sel = S // SEL_LEN
    assert D == 1024 and S % 2048 == 0 and n_sel <= SEL_SLOTS
    topn = min(SEL_TOPN, n_sel)
    tm_in, tm_out, tq, tk = 1024, 512, 512, 512

    mod = _modulation(c, w_mod, b_mod)

    w_b = w_in.astype(BF16)
    w_g = _gate_cols(w_in[:, :, C_END:]).astype(BF16)
    gq = jnp.tile(q_norm, (1, N_HEADS))[:, None, :]
    gk = jnp.stack([jnp.tile(k_norm[:, 1], (1, 2)), jnp.tile(k_norm[:, 2], (1, 2))], axis=1)
    pad64 = ((0, 0), (0, HEAD_DIM))
    kn = jnp.stack([jnp.pad(k_norm[:, 0], pad64), jnp.pad(_swap_halves(k_norm[:, 0]), pad64)], axis=1)
    w1_k = jnp.pad(cmp_w1[:, 0].astype(BF16), ((0, 0), (0, 0), (0, HEAD_DIM)))
    w1_v = jnp.pad(cmp_w1[:, 1].astype(BF16), ((0, 0), (0, 0), (HEAD_DIM, 0)))
    w1c = jnp.stack([w1_k.reshape(depth, CMP_LEN, HEAD_DIM, LANES),
                     w1_v.reshape(depth, CMP_LEN, HEAD_DIM, LANES)],
                    axis=2).reshape(depth, CMP_LEN, LANES, LANES)
    zero_w2 = jnp.zeros_like(cmp_w2[:, 0])
    w2c = jnp.concatenate([
        jnp.concatenate([cmp_w2[:, 0], zero_w2, _rot_cols(cmp_w2[:, 0]), zero_w2, zero_w2, zero_w2], axis=-1),
        jnp.concatenate([zero_w2, zero_w2, zero_w2, zero_w2, cmp_w2[:, 1], zero_w2], axis=-1)],
        axis=-2).astype(BF16)
    pe = jnp.concatenate([cmp_pe[:, 0], cmp_pe[:, 1]], axis=-1)
    pool_wb = pool_w.astype(BF16)
    wo_b, w1_b, w2_b = w_out.astype(BF16), w_ff1.astype(BF16), w_ff2.astype(BF16)

    cos, sin = _rope_tables(jnp.arange(S), 2, signed=True)
    cos_c, sin_c = _rope_tables(jnp.arange(n_chunks) * CMP_STRIDE + (CMP_LEN - 1), 2, signed=False)
    bd = jnp.asarray(np.kron(np.eye(N_HEADS, dtype=np.float32),
                             np.full((HEAD_DIM, HEAD_DIM), 1.0 / HEAD_DIM, np.float32)), BF16)
    ov = jnp.asarray(_overlap(n_chunks, n_sel), BF16)

    for l in range(depth):
        y_pool, qt, keys, vst, vwt, kvc, gt = _inproj(
            x, mod, l, norm1[:, None, :], w_b, w_g, cos, sin, gq, gk, bd,
            pool_wb, pool_scale[:, None, :], tm_in)
        kcmp, vcmpt = _compress(kvc, l, pe, w1c, w2c, kn, cos_c, sin_c)
        y_nsa = _nsa(qt, kcmp, vcmpt, ov, keys.reshape(B, KV_GROUPS, 2 * S, LANES), vst, vwt, gt,
                     tq, tk, topn)
        x = _out_ffn(x, mod, l, norm2[:, None, :], y_pool, y_nsa, wo_b, w1_b, w2_b, tm_out)
    return x
```

```python
import functools
import math

import numpy as np
import jax
import jax.numpy as jnp
from jax import lax
from jax.experimental import pallas as pl
from jax.experimental.pallas import tpu as pltpu

F32 = jnp.float32
BF16 = jnp.bfloat16

LANES = 128
SUBLANES = 8
HEAD_DIM = 64
N_HEADS = 8
KV_GROUPS = 2
HEADS_PER_GROUP = N_HEADS // KV_GROUPS
N_BRANCH = 3
GATE_ROWS = 16
POOL_WINDOWS = (2, 4, 8, 16)
POOL_HALO = 16
CMP_LEN = 32
CMP_STRIDE = 16
SEL_LEN = 64
SEL_SLOTS = 64
SEL_TOPN = 16
SUB_Q = 128
WINDOW = 512
ROPE_THETA = 10000.0
EPS = 1e-6
NEG = -1e30
FORCED_SCORE = 1e4
LOG2E = math.log2(math.e)
SCALE = 1.0 / math.sqrt(HEAD_DIM)
VMEM_LIMIT = 56 * 1024 * 1024
MXU_COLS = 256


def _dot(a, b):
    return jnp.dot(a, b, preferred_element_type=F32)


def _split_bf16(v):
    hi = v.astype(BF16)
    lo = (v - hi.astype(F32)).astype(BF16)
    return hi, lo


def _mod_kernel(c_ref, w_ref, b_ref, o_ref, *, batch):
    c = c_ref[...]
    rows = c.shape[0]
    ca = c * jax.nn.sigmoid(c)
    hi = ca.astype(BF16).astype(F32)
    row = lax.broadcasted_iota(jnp.int32, ca.shape, 0)
    a = jnp.where(row < batch, hi, ca - hi).astype(BF16)
    w = w_ref[0]
    w_hi, w_lo = _split_bf16(w)
    r1 = _dot(a, w_hi)
    r2 = _dot(a, w_lo)
    o_ref[0] = r1 + pltpu.roll(r1, rows - batch, axis=0) + r2 + b_ref[0]


def _modulation(c, w_mod, b_mod):
    depth, d, n = w_mod.shape
    batch = c.shape[0]
    rows = -(-2 * batch // SUBLANES) * SUBLANES
    c2 = jnp.concatenate([c, c, jnp.zeros((rows - 2 * batch, d), F32)], axis=0)
    tn = 1536
    out = pl.pallas_call(
        functools.partial(_mod_kernel, batch=batch),
        grid=(depth, n // tn),
        in_specs=[pl.BlockSpec((rows, d), lambda l, j: (0, 0)),
                  pl.BlockSpec((1, d, tn), lambda l, j: (l, 0, j)),
                  pl.BlockSpec((1, 1, tn), lambda l, j: (l, 0, j))],
        out_specs=pl.BlockSpec((1, rows, tn), lambda l, j: (l, 0, j)),
        out_shape=jax.ShapeDtypeStruct((depth, rows, n), F32),
        compiler_params=pltpu.CompilerParams(vmem_limit_bytes=VMEM_LIMIT),
    )(c2, w_mod, b_mod.reshape(depth, 1, n))
    return out[:, :batch].reshape(depth, batch, 6, d)


C_U = 0
C_Q = 512
C_KV = 1024
C_END = 1792


def _inproj_kernel(x_ref, mod_ref, n1_ref, w_ref, wg_ref, cos_ref, sin_ref, gq_ref, gk_ref, bd_ref,
                   pw_ref, psc_ref,
                   yp_ref, qt_ref, kk_ref, vst_ref, vwt_ref, kvc_ref, gt_ref, ext_sc, *, tm):
    i = pl.program_id(1)
    shift = mod_ref[0, 0, 0:1, :]
    scale = mod_ref[0, 0, 1:2, :]

    x = x_ref[0]
    ms = jnp.mean(x * x, axis=-1, keepdims=True)
    hb = ((x * lax.rsqrt(ms + EPS) * n1_ref[...]) * (1.0 + scale) + shift).astype(BF16)

    def proj(a, b):
        return _dot(hb, w_ref[:, a:b])

    def seg_mean(v, bdm):
        return _dot((v * v).astype(BF16), bdm)

    def rope(y, cos_t, sin_t):
        width = y.shape[1]
        first = (lax.broadcasted_iota(jnp.int32, y.shape, 1) & (HEAD_DIM // 2)) == 0
        rot = jnp.where(first, pltpu.roll(y, width - HEAD_DIM // 2, axis=1),
                        pltpu.roll(y, HEAD_DIM // 2, axis=1))
        return y * cos_t + rot * sin_t

    @pl.when(i == 0)
    def _():
        ext_sc[0:POOL_HALO, :] = jnp.zeros((POOL_HALO, C_Q), F32)

    @pl.when(i > 0)
    def _():
        ext_sc[0:POOL_HALO, :] = ext_sc[tm:tm + POOL_HALO, :]

    ext_sc[POOL_HALO:POOL_HALO + tm, :] = proj(C_U, C_Q)
    q = proj(C_Q, C_KV)
    kv = proj(C_KV, C_END)
    gates = _dot(hb, wg_ref[...])
    pos = i * tm + lax.broadcasted_iota(jnp.int32, (tm, LANES), 0)
    for g, w in enumerate(POOL_WINDOWS):
        c0, c1 = g * LANES, (g + 1) * LANES
        cur = ext_sc[POOL_HALO:POOL_HALO + tm, c0:c1]
        acc = cur
        for k in range(1, w):
            acc = acc + ext_sc[POOL_HALO - k:POOL_HALO - k + tm, c0:c1]
        cnt = jnp.minimum(pos + 1, w).astype(F32)
        pooled = acc / cnt - cur
        yp_ref[0, :, c0:c1] = (_dot(pooled.astype(BF16), pw_ref[g]) * psc_ref[:, c0:c1]).astype(BF16)

    cos = cos_ref[...]
    sin = sin_ref[...]
    cos4 = jnp.concatenate([cos] * 4, axis=1)
    sin4 = jnp.concatenate([sin] * 4, axis=1)
    rq = lax.rsqrt(seg_mean(q, bd_ref[...]) + EPS)
    qo = rope(q * gq_ref[...], cos4, sin4) * (rq * (SCALE * LOG2E))
    for hp in range(N_HEADS // 2):
        qt_ref[0, hp] = qo[:, LANES * hp:LANES * (hp + 1)].T.astype(BF16)

    bd2 =bd_ref[0:LANES, 0:LANES]
    k_c, k_s, k_w = kv[:, 0:128], kv[:, 128:256], kv[:, 256:384]
    v_c, v_s, v_w = kv[:, 384:512], kv[:, 512:640], kv[:, 640:768]
    kso = rope(k_s * gk_ref[0:1, :], cos, sin) * lax.rsqrt(seg_mean(k_s, bd2) + EPS)
    kwo = rope(k_w * gk_ref[1:2, :], cos, sin) * lax.rsqrt(seg_mean(k_w, bd2) + EPS)

    lane = lax.broadcasted_iota(jnp.int32, (tm, LANES), 1)
    low = lane < HEAD_DIM
    pos = i * tm + lax.broadcasted_iota(jnp.int32, (tm, LANES), 0)
    blk = pos // SEL_LEN
    blk_flag = jnp.where(lane - HEAD_DIM == blk, -NEG, 0.0)
    ones_col = jnp.where(lane == HEAD_DIM, 1.0, 0.0)
    for g in range(KV_GROUPS):
        def grp(a, g=g):
            return pltpu.roll(a, HEAD_DIM, axis=1) if g else a
        kk_ref[0, g, 0] = jnp.where(low, grp(kwo), 0.0).astype(BF16)
        kk_ref[0, g, 1] = jnp.where(low, grp(kso), blk_flag).astype(BF16)
        vst_ref[0, g] = jnp.where(low, grp(v_s), ones_col).T.astype(BF16)
        vwt_ref[0, g] = jnp.where(low, grp(v_w), ones_col).T.astype(BF16)
        kvc_ref[0, g] = jnp.where(low, grp(k_c), v_c if g else pltpu.roll(v_c, HEAD_DIM, axis=1))
    gt_ref[0] = jax.nn.sigmoid(gates).T[0:KV_GROUPS * GATE_ROWS]


def _inproj(x, mod, layer, n1, w, wg, cos, sin, gq, gk, bd, pw, psc, tm):
    B, S, D = x.shape
    kern = functools.partial(_inproj_kernel, tm=tm)
    grp_rows = lambda width: pl.BlockSpec((1, KV_GROUPS, tm, width), lambda b, i: (b, 0, i, 0))
    grp_cols = pl.BlockSpec((1, KV_GROUPS, LANES, tm), lambda b, i: (b, 0, 0, i))
    rows_shape = lambda width: jax.ShapeDtypeStruct((B, KV_GROUPS, S, width), BF16)
    cols_shape = jax.ShapeDtypeStruct((B, KV_GROUPS, LANES, S), BF16)
    const = lambda shape: pl.BlockSpec(shape, lambda b, i: (0,) * len(shape))
    per_layer = lambda a: pl.BlockSpec((None,) + a.shape[1:], lambda b, i: (layer,) + (0,) * (a.ndim - 1))
    return pl.pallas_call(
        kern,
        grid=(B, S // tm),
        in_specs=[pl.BlockSpec((1, tm, D), lambda b, i: (b, i, 0)),
                  pl.BlockSpec((1, 1, 6, D), lambda b, i: (layer, b, 0, 0)),
                  per_layer(n1), per_layer(w), per_layer(wg),
                  pl.BlockSpec((tm, LANES), lambda b, i: (i, 0)),
                  pl.BlockSpec((tm, LANES), lambda b, i: (i, 0)),
                  per_layer(gq), per_layer(gk), const(bd.shape),
                  per_layer(pw), per_layer(psc)],
        out_specs=[pl.BlockSpec((1, tm, C_Q), lambda b, i: (b, i, 0)),
                   pl.BlockSpec((1, N_HEADS // 2, LANES, tm), lambda b, i: (b, 0, 0, i)),
                   pl.BlockSpec((1, KV_GROUPS, 2, tm, LANES), lambda b, i: (b, 0, 0, i, 0)),
                   grp_cols, grp_cols,
                   grp_rows(LANES),
                   pl.BlockSpec((1, KV_GROUPS * GATE_ROWS, tm), lambda b, i: (b, 0, i))],
        out_shape=[jax.ShapeDtypeStruct((B, S, C_Q), BF16),
                   jax.ShapeDtypeStruct((B, N_HEADS // 2, LANES, S), BF16),
                   jax.ShapeDtypeStruct((B, KV_GROUPS, 2, S, LANES), BF16), cols_shape, cols_shape,
                   jax.ShapeDtypeStruct((B, KV_GROUPS, S, LANES), F32),
                   jax.ShapeDtypeStruct((B, KV_GROUPS * GATE_ROWS, S), F32)],
        scratch_shapes=[pltpu.VMEM((tm + POOL_HALO, C_Q), F32)],
        compiler_params=pltpu.CompilerParams(vmem_limit_bytes=VMEM_LIMIT),
    )(x, mod, n1, w, wg, cos, sin, gq, gk, bd, pw, psc)


def _compress_kernel(x_ref, pe_ref, w1_ref, w2_ref, kn_ref, cos_ref, sin_ref, ko_ref, vot_ref):
    nc = ko_ref.shape[1]
    taps = CMP_LEN // 2
    first = jnp.zeros((nc, LANES), F32)
    second = jnp.zeros((nc, LANES), F32)
    for t in range(taps):
        x = x_ref[0, 0, pl.ds(t, nc, stride=CMP_STRIDE), :]
        first = first + _dot((x + pe_ref[t:t + 1, :]).astype(BF16), w1_ref[t])
        second = second + _dot((x + pe_ref[taps + t:taps + t + 1, :]).astype(BF16), w1_ref[taps + t])
    pre = first + pltpu.roll(second, nc - 1, axis=0)
    h = (pre * jax.nn.sigmoid(pre)).astype(BF16)
    kkv = _dot(h, w2_ref[...])
    k = kkv[:, 0:LANES]
    kr = kkv[:, LANES:2 * LANES]
    r = lax.rsqrt(jnp.sum(k * k, axis=-1, keepdims=True) * (1.0 / HEAD_DIM) + EPS)
    ko = (k * kn_ref[0:1, :] * cos_ref[...] + kr * kn_ref[1:2, :] * sin_ref[...]) * r
    ko_ref[0] = ko.astype(BF16)
    vot_ref[0] = kkv[:, 2 * LANES:3 * LANES].T.astype(BF16)


def _compress(kvc, layer, pe, w1, w2, kn, cos_c, sin_c):
    B, G, S, _ = kvc.shape
    nc = S // CMP_STRIDE
    const = lambda shape: pl.BlockSpec(shape, lambda b, g: (0,) * len(shape))
    per_layer = lambda a: pl.BlockSpec((None,) + a.shape[1:], lambda b, g: (layer,) + (0,) * (a.ndim - 1))
    return pl.pallas_call(
        _compress_kernel,
        grid=(B, G),
        in_specs=[pl.BlockSpec((1, 1, S, LANES), lambda b, g: (b, g, 0, 0)),
                  per_layer(pe), per_layer(w1), per_layer(w2), per_layer(kn),
                  const(cos_c.shape), const(sin_c.shape)],
        out_specs=[pl.BlockSpec((1, nc, LANES), lambda b, g: (b * G + g, 0, 0)),
                   pl.BlockSpec((1, LANES, nc), lambda b, g: (b * G + g, 0, 0))],
        out_shape=[jax.ShapeDtypeStruct((B * G, nc, LANES), BF16),
                   jax.ShapeDtypeStruct((B * G, LANES, nc), BF16)],
        compiler_params=pltpu.CompilerParams(vmem_limit_bytes=VMEM_LIMIT),
    )(kvc, pe, w1, w2, kn, cos_c, sin_c)


def _nsa_kernel(qt_ref, kc_ref, vct_ref, ov_ref, k_ref, vst_ref, vwt_ref, gt_ref,
                y_ref, q_sc, acc_sc, o_sc, w_sc, s_sc, p_sc, *, tq, tk, topn):
    i = pl.program_id(2)
    seq = k_ref.shape[2] // 2
    nsub = tq // SUB_Q
    wsub = HEADS_PER_GROUP * SUB_Q
    cols = nsub * wsub
    nc = kc_ref.shape[1]

    def lanes(jq, hh):
        return slice(jq * wsub + hh * SUB_Q, jq * wsub + (hh + 1) * SUB_Q)

    def toks(jq):
        return slice(jq * SUB_Q, (jq + 1) * SUB_Q)

    for jq in range(nsub):
        for hh in range(HEADS_PER_GROUP):
            q_sc[0:HEAD_DIM, lanes(jq, hh)] = qt_ref[
                0, hh // 2, (hh % 2) * HEAD_DIM:(hh % 2 + 1) * HEAD_DIM, toks(jq)]
    q_sc[HEAD_DIM:LANES, :] = jnp.full((LANES - HEAD_DIM, cols), -1.0, BF16)

    def tok(shape, lane0=0):
        lane = lane0 + lax.broadcasted_iota(jnp.int32, shape, 1)
        return i * tq + (lane // wsub) * SUB_Q + (lane & (SUB_Q - 1))

    nwin = WINDOW // SUB_Q
    kk = lax.broadcasted_iota(jnp.int32, (SUB_Q, wsub), 0)
    r = lax.broadcasted_iota(jnp.int32, (SUB_Q, wsub), 1) & (SUB_Q - 1)

    def chunk(jq, c):
        return i * nsub + jq - nwin + c

    def window_scores(jq):
        q_sub = q_sc[:, jq * wsub:(jq + 1) * wsub]
        for c in range(nwin + 1):
            ci = chunk(jq, c)
            start = pl.multiple_of(jnp.where(ci >= 0, ci * SUB_Q, seq), SUB_Q)
            sw = _dot(k_ref[0, 0, pl.ds(start, SUB_Q), :], q_sub)
            if c == nwin:
                sw = jnp.where(kk <= r, sw, NEG)
            elif c == 0:
                sw = jnp.where(kk > r, sw, NEG)
            w_sc[jq, c * SUB_Q:(c + 1) * SUB_Q, :] = sw

    def window_finish(jq):
        sw = w_sc[jq]
        pw = jnp.exp2(sw - jnp.max(sw, axis=0, keepdims=True)).astype(BF16)
        ow = jnp.zeros((LANES, wsub), F32)
        for c in range(nwin + 1):
            start = pl.multiple_of(jnp.maximum(chunk(jq, c), 0) * SUB_Q, SUB_Q)
            ow = ow + _dot(vwt_ref[0, 0, :, pl.ds(start, SUB_Q)], pw[c * SUB_Q:(c + 1) * SUB_Q])
        o_sc[1, :, jq * wsub:(jq + 1) * wsub] = ow

    s = _dot(kc_ref[0], q_sc[...])
    for jq in range(nsub):
        window_scores(jq)
    last_block = lax.shift_right_arithmetic(tok((1, cols)) - (CMP_LEN - 1), CMP_STRIDE.bit_length() - 1)
    s = jnp.where(lax.broadcasted_iota(jnp.int32, (nc, cols), 0) <= last_block, s, NEG)
    m = jnp.max(s, axis=0, keepdims=True)
    p = jnp.exp2(s - jnp.maximum(m, 0.5 * NEG))
    l = jnp.sum(p, axis=0, keepdims=True)
    inv_l = 1.0 / jnp.where(l > 0.0, l, 1.0)
    o_sc[0] = _dot(vct_ref[0], p.astype(BF16)) * inv_l

    sums = []
    for jq in range(nsub):
        ps = p[:, lanes(jq, 0)] * inv_l[:, lanes(jq, 0)]
        for hh in range(1, HEADS_PER_GROUP):
            ps = ps + p[:, lanes(jq, hh)] * inv_l[:, lanes(jq, hh)]
        sums.append(ps)
    hi, lo = _split_bf16(jnp.concatenate(sums, axis=1))
    ov = ov_ref[...]
    imp = _dot(ov, hi) + _dot(ov, lo)

    for jq in range(nsub - 1):
        window_finish(jq)

    j = lax.broadcasted_iota(jnp.int32, (SEL_SLOTS, tq), 0)
    cur = (i * tq + lax.broadcasted_iota(jnp.int32, (SEL_SLOTS, tq), 1)) // SEL_LEN
    forced = (j == 0) | (j == cur) | (j == cur - 1)
    imp = jnp.where(j <= cur, jnp.where(forced, FORCED_SCORE, imp), -1.0)
    def ranked(n_rows):
        def run():
            sub = SUBLANES
            slabs = [imp[a:a + sub] for a in range(0, n_rows, sub)]
            ranks = [jnp.zeros((sub, tq), F32) for _ in slabs]
            sub_row = lax.broadcasted_iota(jnp.int32, (sub, tq), 0)
            for jj in range(n_rows):
                rowj = jnp.broadcast_to(imp[jj:jj + 1, :], (sub, tq))
                for a, xs in enumerate(slabs):
                    if a * sub > jj:
                        inc = jnp.where(rowj >= xs, 1.0, 0.0)
                    elif a * sub + sub - 1 <= jj:
                        inc = jnp.where(rowj > xs, 1.0, 0.0)
                    else:
                        tie = jnp.where(sub_row > jj - a * sub, 1.0, 0.0)
                        inc = jnp.where(rowj > xs, 1.0, jnp.where(rowj == xs, tie, 0.0))
                    ranks[a] = ranks[a] + inc
            rank = jnp.concatenate(ranks, axis=0)
            kept = jnp.where((rank < float(topn)) & (imp[0:n_rows] >= 0.0), 0.0, -1.0)
            if n_rows == SEL_SLOTS:
                return kept
            return jnp.concatenate([kept, jnp.full((SEL_SLOTS - n_rows, tq), -1.0, F32)], axis=0)
        return run

    row_options = tuple(r for r in (SEL_SLOTS // 2, 3 * SEL_SLOTS // 4) if r > topn) + (SEL_SLOTS,)
    visible = (i + 1) * tq // SEL_LEN
    branch = sum((visible > r).astype(jnp.int32) for r in (topn,) + row_options[:-1])
    flag = lax.switch(branch, [lambda: jnp.where(imp >= 0.0, 0.0, -1.0)]
                      + [ranked(r) for r in row_options]).astype(BF16)

    for jq in range(nsub):
        for hh in range(HEADS_PER_GROUP):
            q_sc[HEAD_DIM:LANES, lanes(jq, hh)] = flag[:, toks(jq)]
    acc_sc[...] = jnp.zeros(acc_sc.shape, F32)

    blocks = [slice(a, a + MXU_COLS) for a in range(0, cols, MXU_COLS)]

    def key_tile(kt):
        return k_ref[0, 0, pl.ds(pl.multiple_of(seq + kt * tk, tk), tk), :]

    def value_tile(kt):
        return vst_ref[0, 0, :, pl.ds(pl.multiple_of(jnp.maximum(kt, 0) * tk, tk), tk)]

    def score_block(k, slot, cs):
        ss = _dot(k, q_sc[:, cs])
        s_sc[slot, :, cs] = ss
        return jnp.max(ss, axis=0, keepdims=True)

    def softmax_block(slot, cs, m_prev, m_blk, diagonal):
        ss = s_sc[slot, :, cs]
        if diagonal:
            m_blk = jnp.max(ss, axis=0, keepdims=True)
        m_new = jnp.maximum(m_prev, m_blk)
        p_sc[slot, :, cs] = jnp.exp2(ss - m_new).astype(BF16)
        return m_new, jnp.exp2(m_prev - m_new)

    def value_block(vt, slot, cs, a_blk):
        acc_sc[:, cs] = acc_sc[:, cs] * a_blk + _dot(vt, p_sc[slot, :, cs])

    def row(parts):
        return jnp.concatenate(parts, axis=1)

    n_tiles = (i * tq) // tk + 1
    k_first = key_tile(0)
    m_first = row([score_block(k_first, 0, cs) for cs in blocks])
    p_sc[1] = jnp.zeros(p_sc.shape[1:], BF16)
    window_finish(nsub - 1)

    def steady(k, carry):
        m_run, alpha, m_tile = carry

        def arm(slot):
            kx, vt = key_tile(k), value_tile(k - 2)
            m_new, a_new, m_next = [], [], []
            for cs in blocks:
                mn, an = softmax_block(1 - slot, cs, m_run[:, cs], m_tile[:, cs], False)
                m_next.append(score_block(kx, slot, cs))
                value_block(vt, slot, cs, alpha[:, cs])
                m_new.append(mn)
                a_new.append(an)
            return row(m_new), row(a_new), row(m_next)

        return lax.cond(k % 2 == 0, lambda: arm(0), lambda: arm(1))

    m_run, alpha, m_tile = lax.fori_loop(
        1, n_tiles, steady,
        (jnp.full((1, cols), NEG, F32), jnp.ones((1, cols), F32), m_first))

    def drain(slot):
        row0 = i * tq - (n_tiles - 1) * tk
        vt_prev, vt_last = value_tile(n_tiles - 2), value_tile(n_tiles - 1)
        for cs in blocks:
            value_block(vt_prev, 1 - slot, cs, alpha[:, cs])
        for jq in range(nsub):
            rows = pl.ds(pl.multiple_of(row0 + jq * SUB_Q, SUB_Q), SUB_Q)
            own = slice(jq * wsub, (jq + 1) * wsub)
            s_sc[slot, rows, own] = jnp.where(kk <= r, s_sc[slot, rows, own], NEG)
        for cs in blocks:
            _, a_last = softmax_block(slot, cs, m_run[:, cs], m_tile[:, cs], True)
            value_block(vt_last, slot, cs, a_last)

    @pl.when((n_tiles - 1) % 2 == 0)
    def _():
        drain(0)

    @pl.when((n_tiles - 1) % 2 == 1)
    def _():
        drain(1)

    gt = gt_ref[0]
    tiles = []
    for jq in range(nsub):
        heads = []
        for hh in range(HEADS_PER_GROUP):
            cs = lanes(jq, hh)
            g_c = gt[hh:hh + 1, toks(jq)]
            g_s = gt[HEADS_PER_GROUP + hh:HEADS_PER_GROUP + hh + 1, toks(jq)]
            g_w = gt[2 * HEADS_PER_GROUP + hh:2 * HEADS_PER_GROUP + hh + 1, toks(jq)]
            heads.append(o_sc[0, 0:HEAD_DIM, cs] * g_c
                         + acc_sc[0:HEAD_DIM, cs] * (g_s / acc_sc[HEAD_DIM:HEAD_DIM + 1, cs])
                         + o_sc[1, 0:HEAD_DIM, cs] * (g_w / o_sc[1, HEAD_DIM:HEAD_DIM + 1, cs]))
        tiles.append(jnp.concatenate(heads, axis=0).T)
    y_ref[0] = jnp.concatenate(tiles, axis=0).astype(BF16)


def _nsa(qt, kcmp, vcmpt, ov, keys, vst, vwt, gt, tq, tk, topn):
    B, _, _, S = qt.shape
    nc = kcmp.shape[1]
    cols = HEADS_PER_GROUP * tq
    kern = functools.partial(_nsa_kernel, tq=tq, tk=tk, topn=topn)
    seq_rows = pl.BlockSpec((1, 1, 2 * S, LANES), lambda b, g, i: (b, g, 0, 0))
    seq_cols = pl.BlockSpec((1, 1, LANES, S), lambda b, g, i: (b, g, 0, 0))
    return pl.pallas_call(
        kern,
        grid=(B, KV_GROUPS, S // tq),
        in_specs=[pl.BlockSpec((1, HEADS_PER_GROUP // 2, LANES, tq), lambda b, g, i: (b, g, 0, i)),
                  pl.BlockSpec((1, nc, LANES), lambda b, g, i: (b * KV_GROUPS + g, 0, 0)),
                  pl.BlockSpec((1, LANES, nc), lambda b, g, i: (b * KV_GROUPS + g, 0, 0)),
                  pl.BlockSpec((SEL_SLOTS, nc), lambda b, g, i: (0, 0)),
                  seq_rows, seq_cols, seq_cols,
                  pl.BlockSpec((1, GATE_ROWS, tq), lambda b, g, i: (b, g, i))],
        out_specs=pl.BlockSpec((1, tq, HEADS_PER_GROUP * HEAD_DIM), lambda b, g, i: (b, i, g)),
        out_shape=jax.ShapeDtypeStruct((B, S, N_HEADS * HEAD_DIM), BF16),
        scratch_shapes=[pltpu.VMEM((LANES, cols), BF16), pltpu.VMEM((LANES, cols), F32),
                        pltpu.VMEM((2, LANES, cols), F32),
                        pltpu.VMEM((tq // SUB_Q, WINDOW + SUB_Q, HEADS_PER_GROUP * SUB_Q), F32),
                        pltpu.VMEM((2, tk, cols), F32), pltpu.VMEM((2, tk, cols), BF16)],
        compiler_params=pltpu.CompilerParams(vmem_limit_bytes=VMEM_LIMIT),
    )(qt, kcmp, vcmpt, ov, keys, vst, vwt, gt)


def _out_ffn_kernel(x_ref, mod_ref, n2_ref, yp_ref, yn_ref, wo_ref, w1_ref, w2_ref, o_ref, *, ff_chunk):
    x = x_ref[0]
    gate1 = mod_ref[0, 0, 2:3, :]
    shift2 = mod_ref[0, 0, 3:4, :]
    scale2 = mod_ref[0, 0, 4:5, :]
    gate2 = mod_ref[0, 0, 5:6, :]
    half = wo_ref.shape[0] // 2
    mix = _dot(yp_ref[0], wo_ref[0:half, :]) + _dot(yn_ref[0], wo_ref[half:2 * half, :])
    x1 = x + gate1 * mix
    ms = jnp.mean(x1 * x1, axis=-1, keepdims=True)
    h2 = ((x1 * lax.rsqrt(ms + EPS) * n2_ref[...]) * (1.0 + scale2) + shift2).astype(BF16)
    d_ff = w1_ref.shape[1]
    ff = jnp.zeros(x.shape, F32)
    for c in range(d_ff // ff_chunk):
        a = jnp.maximum(_dot(h2, w1_ref[:, c * ff_chunk:(c + 1) * ff_chunk]), 0.0)
        ff = ff + _dot((a * a).astype(BF16), w2_ref[c * ff_chunk:(c + 1) * ff_chunk, :])
    o_ref[0] = x1 + gate2 * ff


def _out_ffn(x, mod, layer, n2, yp, yn, wo, w1, w2, tm):
    B, S, D = x.shape
    kern = functools.partial(_out_ffn_kernel, ff_chunk=1024)
    tok = lambda width: pl.BlockSpec((1, tm, width), lambda b, i: (b, i, 0))
    per_layer = lambda a: pl.BlockSpec((None,) + a.shape[1:], lambda b, i: (layer,) + (0,) * (a.ndim - 1),
                                       pipeline_mode=pl.Buffered(1))
    return pl.pallas_call(
        kern,
        grid=(B, S // tm),
        in_specs=[tok(D),
                  pl.BlockSpec((1, 1, 6, D), lambda b, i: (layer, b, 0, 0)),
                  per_layer(n2),
                  tok(512), tok(512),
                  per_layer(wo), per_layer(w1), per_layer(w2)],
        out_specs=tok(D),
        out_shape=jax.ShapeDtypeStruct((B, S, D), F32),
        compiler_params=pltpu.CompilerParams(vmem_limit_bytes=VMEM_LIMIT),
    )(x, mod, n2, yp, yn, wo, w1, w2)


def _rot_cols(w):
    shp = w.shape
    w4 = w.reshape(shp[:-1] + (shp[-1] // HEAD_DIM, 2, HEAD_DIM // 2))
    return jnp.stack([-w4[..., 1, :], w4[..., 0, :]], axis=-2).reshape(shp)


def _swap_halves(g):
    half = HEAD_DIM // 2
    return jnp.concatenate([g[..., half:], g[..., :half]], axis=-1)


def _rope_tables(pos, reps, signed):
    half = HEAD_DIM // 2
    inv = ROPE_THETA ** (-jnp.arange(half, dtype=F32) * 2.0 / HEAD_DIM)
    ang = pos.astype(F32)[:, None] * inv[None, :]
    cos = jnp.tile(jnp.cos(ang), (1, 2 * reps))
    sin = jnp.sin(ang)
    sin = jnp.tile(jnp.concatenate([-sin if signed else sin, sin], axis=1), (1, reps))
    return cos, sin


def _overlap(n_chunks, n_sel):
    cs0 = np.arange(n_chunks) * CMP_STRIDE
    ss0 = np.arange(n_sel) * SEL_LEN
    ov = np.minimum(cs0[:, None] + CMP_LEN, ss0[None, :] + SEL_LEN) - np.maximum(cs0[:, None], ss0[None, :])
    ov = np.clip(ov, 0, None).astype(np.float32) / CMP_LEN
    out = np.zeros((SEL_SLOTS, n_chunks), np.float32)
    out[:n_sel] = ov.T
    return out


def _gate_cols(w_g):
    lead = w_g.shape[:-1]
    by_head = w_g.reshape(lead + (KV_GROUPS, HEADS_PER_GROUP, N_BRANCH))
    by_branch = jnp.swapaxes(by_head, -1, -2).reshape(lead + (KV_GROUPS, N_BRANCH * HEADS_PER_GROUP))
    pad_rows = GATE_ROWS - N_BRANCH * HEADS_PER_GROUP
    grouped = jnp.pad(by_branch, [(0, 0)] * len(lead) + [(0, 0), (0, pad_rows)])
    flat = grouped.reshape(lead + (KV_GROUPS * GATE_ROWS,))
    return jnp.pad(flat, [(0, 0)] * len(lead) + [(0, LANES - KV_GROUPS * GATE_ROWS)])


def kernel(x, c, w_mod, b_mod, norm1, norm2, w_in, pool_w, pool_scale, q_norm, k_norm, cmp_pe, cmp_w1, cmp_w2, w_out, w_ff1, w_ff2):
    B, S, D = x.shape
    depth = w_mod.shape[0]
    n_chunks = S // CMP_STRIDE
    n_sel = S // SEL_LEN
    assert D == 1024 and S % 2048 == 0 and n_sel <= SEL_SLOTS
    topn = min(SEL_TOPN, n_sel)
    tm_in, tm_out, tq, tk = 1024, 1024, 512, 512

    mod = _modulation(c, w_mod, b_mod)

    w_b = w_in[:, :, :C_END].astype(BF16)
    w_g = _gate_cols(w_in[:, :, C_END:]).astype(BF16)
    gq = jnp.tile(q_norm, (1, N_HEADS))[:, None, :]
    gk = jnp.stack([jnp.tile(k_norm[:, 1], (1, 2)), jnp.tile(k_norm[:, 2], (1, 2))], axis=1)
    pad64 = ((0, 0), (0, HEAD_DIM))
    kn = jnp.stack([jnp.pad(k_norm[:, 0], pad64), jnp.pad(_swap_halves(k_norm[:, 0]), pad64)], axis=1)
    w1_k = jnp.pad(cmp_w1[:, 0].astype(BF16), ((0, 0), (0, 0), (0, HEAD_DIM)))
    w1_v = jnp.pad(cmp_w1[:, 1].astype(BF16), ((0, 0), (0, 0), (HEAD_DIM, 0)))
    w1c = jnp.stack([w1_k.reshape(depth, CMP_LEN, HEAD_DIM, LANES),
                     w1_v.reshape(depth, CMP_LEN, HEAD_DIM, LANES)],
                    axis=2).reshape(depth, CMP_LEN, LANES, LANES)
    zero_w2 = jnp.zeros_like(cmp_w2[:, 0])
    w2c = jnp.concatenate([
        jnp.concatenate([cmp_w2[:, 0], zero_w2, _rot_cols(cmp_w2[:, 0]), zero_w2, zero_w2, zero_w2], axis=-1),
        jnp.concatenate([zero_w2, zero_w2, zero_w2, zero_w2, cmp_w2[:, 1], zero_w2], axis=-1)],
        axis=-2).astype(BF16)
    pe = jnp.concatenate([cmp_pe[:, 0], cmp_pe[:, 1]], axis=-1)
    pool_wb = pool_w.astype(BF16)
    wo_b, w1_b, w2_b = w_out.astype(BF16), w_ff1.astype(BF16), w_ff2.astype(BF16)

    cos, sin = _rope_tables(jnp.arange(S), 2, signed=True)
    cos_c, sin_c = _rope_tables(jnp.arange(n_chunks) * CMP_STRIDE + (CMP_LEN - 1), 2, signed=False)
    bd = jnp.asarray(np.kron(np.eye(N_HEADS, dtype=np.float32),
                             np.full((HEAD_DIM, HEAD_DIM), 1.0 / HEAD_DIM, np.float32)), BF16)
    ov = jnp.asarray(_overlap(n_chunks, n_sel), BF16)

    for l in range(depth):
        y_pool, qt, keys, vst, vwt, kvc, gt = _inproj(
            x, mod, l, norm1[:, None, :], w_b, w_g, cos, sin, gq, gk, bd,
            pool_wb, pool_scale[:, None, :], tm_in)
        kcmp, vcmpt = _compress(kvc, l, pe, w1c, w2c, kn, cos_c, sin_c)
        y_nsa = _nsa(qt, kcmp, vcmpt, ov, keys.reshape(B, KV_GROUPS, 2 * S, LANES), vst, vwt, gt,
                     tq, tk, topn)
        x = _out_ffn(x, mod, l, norm2[:, None, :], y_pool, y_nsa, wo_b, w1_b, w2_b, tm_out)
    return x
```

```python
import functools
import math

import numpy as np
import jax
import jax.numpy as jnp
from jax import lax
from jax.experimental import pallas as pl
from jax.experimental.pallas import tpu as pltpu

F32 = jnp.float32
BF16 = jnp.bfloat16

LANES = 128
SUBLANES = 8
HEAD_DIM = 64
N_HEADS = 8
KV_GROUPS = 2
HEADS_PER_GROUP = N_HEADS // KV_GROUPS
N_BRANCH = 3
GATE_ROWS = 16
POOL_WINDOWS = (2, 4, 8, 16)
POOL_HALO = 16
CMP_LEN = 32
CMP_STRIDE = 16
SEL_LEN = 64
SEL_SLOTS = 64
SEL_TOPN = 16
SUB_Q = 128
WINDOW = 512
ROPE_THETA = 10000.0
EPS = 1e-6
NEG = -1e30
FORCED_SCORE = 1e4
LOG2E = math.log2(math.e)
SCALE = 1.0 / math.sqrt(HEAD_DIM)
VMEM_LIMIT = 56 * 1024 * 1024
MXU_COLS = 256


def _dot(a, b):
    return jnp.dot(a, b, preferred_element_type=F32)


def _split_bf16(v):
    hi = v.astype(BF16)
    lo = (v - hi.astype(F32)).astype(BF16)
    return hi, lo


def _mod_kernel(c_ref, w_ref, b_ref, o_ref, *, batch):
    c = c_ref[...]
    rows = c.shape[0]
    ca = c * jax.nn.sigmoid(c)
    hi = ca.astype(BF16).astype(F32)
    row = lax.broadcasted_iota(jnp.int32, ca.shape, 0)
    a = jnp.where(row < batch, hi, ca - hi).astype(BF16)
    w = w_ref[0]
    w_hi, w_lo = _split_bf16(w)
    r1 = _dot(a, w_hi)
    r2 = _dot(a, w_lo)
    o_ref[0] = r1 + pltpu.roll(r1, rows - batch, axis=0) + r2 + b_ref[0]


def _modulation(c, w_mod, b_mod):
    depth, d, n = w_mod.shape
    batch = c.shape[0]
    rows = -(-2 * batch // SUBLANES) * SUBLANES
    c2 = jnp.concatenate([c, c, jnp.zeros((rows - 2 * batch, d), F32)], axis=0)
    tn = 1536
    out = pl.pallas_call(
        functools.partial(_mod_kernel, batch=batch),
        grid=(depth, n // tn),
        in_specs=[pl.BlockSpec((rows, d), lambda l, j: (0, 0)),
                  pl.BlockSpec((1, d, tn), lambda l, j: (l, 0, j)),
                  pl.BlockSpec((1, 1, tn), lambda l, j: (l, 0, j))],
        out_specs=pl.BlockSpec((1, rows, tn), lambda l, j: (l, 0, j)),
        out_shape=jax.ShapeDtypeStruct((depth, rows, n), F32),
        compiler_params=pltpu.CompilerParams(vmem_limit_bytes=VMEM_LIMIT),
    )(c2, w_mod, b_mod.reshape(depth, 1, n))
    return out[:, :batch].reshape(depth, batch, 6, d)


C_U = 0
C_Q = 512
C_KV = 1024
C_END = 1792


def _inproj_kernel(x_ref, mod_ref, n1_ref, w_ref, wg_ref, cos_ref, sin_ref, gq_ref, gk_ref, bd_ref,
                   pw_ref, psc_ref,
                   yp_ref, qt_ref, kk_ref, vst_ref, vwt_ref, kvc_ref, gt_ref, ext_sc, *, tm):
    i = pl.program_id(1)
    shift = mod_ref[0, 0, 0:1, :]
    scale = mod_ref[0, 0, 1:2, :]

    x = x_ref[0]
    ms = jnp.mean(x * x, axis=-1, keepdims=True)
    hb = ((x * lax.rsqrt(ms + EPS) * n1_ref[...]) * (1.0 + scale) + shift).astype(BF16)

    def proj(a, b):
        return _dot(hb, w_ref[:, a:b])

    def seg_mean(v, bdm):
        return _dot((v * v).astype(BF16), bdm)

    def rope(y, cos_t, sin_t):
        width = y.shape[1]
        first = (lax.broadcasted_iota(jnp.int32, y.shape, 1) & (HEAD_DIM // 2)) == 0
        rot = jnp.where(first, pltpu.roll(y, width - HEAD_DIM // 2, axis=1),
                        pltpu.roll(y, HEAD_DIM // 2, axis=1))
        return y * cos_t + rot * sin_t

    @pl.when(i == 0)
    def _():
        ext_sc[0:POOL_HALO, :] = jnp.zeros((POOL_HALO, C_Q), F32)

    @pl.when(i > 0)
    def _():
        ext_sc[0:POOL_HALO, :] = ext_sc[tm:tm + POOL_HALO, :]

    ext_sc[POOL_HALO:POOL_HALO + tm, :] = proj(C_U, C_Q)
    q = proj(C_Q, C_KV)
    kv = proj(C_KV, C_END)
    gates = _dot(hb, wg_ref[...])
    pos = i * tm + lax.broadcasted_iota(jnp.int32, (tm, LANES), 0)
    for g, w in enumerate(POOL_WINDOWS):
        c0, c1 = g * LANES, (g + 1) * LANES
        cur = ext_sc[POOL_HALO:POOL_HALO + tm, c0:c1]
        acc = cur
        for k in range(1, w):
            acc = acc + ext_sc[POOL_HALO - k:POOL_HALO - k + tm, c0:c1]
        cnt = jnp.minimum(pos + 1, w).astype(F32)
        pooled = acc / cnt - cur
        yp_ref[0, :, c0:c1] = (_dot(pooled.astype(BF16), pw_ref[g]) * psc_ref[:, c0:c1]).astype(BF16)

    cos = cos_ref[...]
    sin = sin_ref[...]
    cos4 = jnp.concatenate([cos] * 4, axis=1)
    sin4 = jnp.concatenate([sin] * 4, axis=1)
    rq = lax.rsqrt(seg_mean(q, bd_ref[...]) + EPS)
    qo = rope(q * gq_ref[...], cos4, sin4) * (rq * (SCALE * LOG2E))
    for hp in range(N_HEADS // 2):
        qt_ref[0, hp] = qo[:, LANES * hp:LANES * (hp + 1)].T.astype(BF16)

    bd2 =bd_ref[0:LANES, 0:LANES]
    k_c, k_s, k_w = kv[:, 0:128], kv[:, 128:256], kv[:, 256:384]
    v_c, v_s, v_w = kv[:, 384:512], kv[:, 512:640], kv[:, 640:768]
    kso = rope(k_s * gk_ref[0:1, :], cos, sin) * lax.rsqrt(seg_mean(k_s, bd2) + EPS)
    kwo = rope(k_w * gk_ref[1:2, :], cos, sin) * lax.rsqrt(seg_mean(k_w, bd2) + EPS)

    lane = lax.broadcasted_iota(jnp.int32, (tm, LANES), 1)
    low = lane < HEAD_DIM
    pos = i * tm + lax.broadcasted_iota(jnp.int32, (tm, LANES), 0)
    blk = pos // SEL_LEN
    blk_flag = jnp.where(lane - HEAD_DIM == blk, -NEG, 0.0)
    ones_col = jnp.where(lane == HEAD_DIM, 1.0, 0.0)
    for g in range(KV_GROUPS):
        def grp(a, g=g):
            return pltpu.roll(a, HEAD_DIM, axis=1) if g else a
        kk_ref[0, g, 0] = jnp.where(low, grp(kwo), 0.0).astype(BF16)
        kk_ref[0, g, 1] = jnp.where(low, grp(kso), blk_flag).astype(BF16)
        vst_ref[0, g] = jnp.where(low, grp(v_s), ones_col).T.astype(BF16)
        vwt_ref[0, g] = jnp.where(low, grp(v_w), ones_col).T.astype(BF16)
        kvc_ref[0, g] = jnp.where(low, grp(k_c), v_c if g else pltpu.roll(v_c, HEAD_DIM, axis=1))
    gt_ref[0] = jax.nn.sigmoid(gates).T[0:KV_GROUPS * GATE_ROWS]


def _inproj(x, mod, layer, n1, w, wg, cos, sin, gq, gk, bd, pw, psc, tm):
    B, S, D = x.shape
    kern = functools.partial(_inproj_kernel, tm=tm)
    grp_rows = lambda width: pl.BlockSpec((1, KV_GROUPS, tm, width), lambda b, i: (b, 0, i, 0))
    grp_cols = pl.BlockSpec((1, KV_GROUPS, LANES, tm), lambda b, i: (b, 0, 0, i))
    rows_shape = lambda width: jax.ShapeDtypeStruct((B, KV_GROUPS, S, width), BF16)
    cols_shape = jax.ShapeDtypeStruct((B, KV_GROUPS, LANES, S), BF16)
    const = lambda shape: pl.BlockSpec(shape, lambda b, i: (0,) * len(shape))
    per_layer = lambda a: pl.BlockSpec((None,) + a.shape[1:], lambda b, i: (layer,) + (0,) * (a.ndim - 1))
    return pl.pallas_call(
        kern,
        grid=(B, S // tm),
        in_specs=[pl.BlockSpec((1, tm, D), lambda b, i: (b, i, 0)),
                  pl.BlockSpec((1, 1, 6, D), lambda b, i: (layer, b, 0, 0)),
                  per_layer(n1), per_layer(w), per_layer(wg),
                  pl.BlockSpec((tm, LANES), lambda b, i: (i, 0)),
                  pl.BlockSpec((tm, LANES), lambda b, i: (i, 0)),
                  per_layer(gq), per_layer(gk), const(bd.shape),
                  per_layer(pw), per_layer(psc)],
        out_specs=[pl.BlockSpec((1, tm, C_Q), lambda b, i: (b, i, 0)),
                   pl.BlockSpec((1, N_HEADS // 2, LANES, tm), lambda b, i: (b, 0, 0, i)),
                   pl.BlockSpec((1, KV_GROUPS, 2, tm, LANES), lambda b, i: (b, 0, 0, i, 0)),
                   grp_cols, grp_cols,
                   grp_rows(LANES),
                   pl.BlockSpec((1, KV_GROUPS * GATE_ROWS, tm), lambda b, i: (b, 0, i))],
        out_shape=[jax.ShapeDtypeStruct((B, S, C_Q), BF16),
                   jax.ShapeDtypeStruct((B, N_HEADS // 2, LANES, S), BF16),
                   jax.ShapeDtypeStruct((B, KV_GROUPS, 2, S, LANES), BF16), cols_shape, cols_shape,
                   jax.ShapeDtypeStruct((B, KV_GROUPS, S, LANES), F32),
                   jax.ShapeDtypeStruct((B, KV_GROUPS * GATE_ROWS, S), F32)],
        scratch_shapes=[pltpu.VMEM((tm + POOL_HALO, C_Q), F32)],
        compiler_params=pltpu.CompilerParams(vmem_limit_bytes=VMEM_LIMIT),
    )(x, mod, n1, w, wg, cos, sin, gq, gk, bd, pw, psc)


def _compress_kernel(x_ref, pe_ref, w1_ref, w2_ref, kn_ref, cos_ref, sin_ref, ko_ref, vot_ref):
    nc = ko_ref.shape[1]
    taps = CMP_LEN // 2
    first = jnp.zeros((nc, LANES), F32)
    second = jnp.zeros((nc, LANES), F32)
    for t in range(taps):
        x = x_ref[0, 0, pl.ds(t, nc, stride=CMP_STRIDE), :]
        first = first + _dot((x + pe_ref[t:t + 1, :]).astype(BF16), w1_ref[t])
        second = second + _dot((x + pe_ref[taps + t:taps + t + 1, :]).astype(BF16), w1_ref[taps + t])
    pre = first + pltpu.roll(second, nc - 1, axis=0)
    h = (pre * jax.nn.sigmoid(pre)).astype(BF16)
    kkv = _dot(h, w2_ref[...])
    k = kkv[:, 0:LANES]
    kr = kkv[:, LANES:2 * LANES]
    r = lax.rsqrt(jnp.sum(k * k, axis=-1, keepdims=True) * (1.0 / HEAD_DIM) + EPS)
    ko = (k * kn_ref[0:1, :] * cos_ref[...] + kr * kn_ref[1:2, :] * sin_ref[...]) * r
    ko_ref[0] = ko.astype(BF16)
    vot_ref[0] = kkv[:, 2 * LANES:3 * LANES].T.astype(BF16)


def _compress(kvc, layer, pe, w1, w2, kn, cos_c, sin_c):
    B, G, S, _ = kvc.shape
    nc = S // CMP_STRIDE
    const = lambda shape: pl.BlockSpec(shape, lambda b, g: (0,) * len(shape))
    per_layer = lambda a: pl.BlockSpec((None,) + a.shape[1:], lambda b, g: (layer,) + (0,) * (a.ndim - 1))
    return pl.pallas_call(
        _compress_kernel,
        grid=(B, G),
        in_specs=[pl.BlockSpec((1, 1, S, LANES), lambda b, g: (b, g, 0, 0)),
                  per_layer(pe), per_layer(w1), per_layer(w2), per_layer(kn),
                  const(cos_c.shape), const(sin_c.shape)],
        out_specs=[pl.BlockSpec((1, nc, LANES), lambda b, g: (b * G + g, 0, 0)),
                   pl.BlockSpec((1, LANES, nc), lambda b, g: (b * G + g, 0, 0))],
        out_shape=[jax.ShapeDtypeStruct((B * G, nc, LANES), BF16),
                   jax.ShapeDtypeStruct((B * G, LANES, nc), BF16)],
        compiler_params=pltpu.CompilerParams(vmem_limit_bytes=VMEM_LIMIT),
    )(kvc, pe, w1, w2, kn, cos_c, sin_c)


def _nsa_kernel(qt_ref, kc_ref, vct_ref, ov_ref, k_ref, vst_ref, vwt_ref, gt_ref,
                y_ref, q_sc, acc_sc, o_sc, w_sc, s_sc, p_sc, *, tq, tk, topn):
    i = pl.program_id(2)
    seq = k_ref.shape[2] // 2
    nsub = tq // SUB_Q
    wsub = HEADS_PER_GROUP * SUB_Q
    cols = nsub * wsub
    nc = kc_ref.shape[1]

    def lanes(jq, hh):
        return slice(jq * wsub + hh * SUB_Q, jq * wsub + (hh + 1) * SUB_Q)

    def toks(jq):
        return slice(jq * SUB_Q, (jq + 1) * SUB_Q)

    for jq in range(nsub):
        for hh in range(HEADS_PER_GROUP):
            q_sc[0:HEAD_DIM, lanes(jq, hh)] = qt_ref[
                0, hh // 2, (hh % 2) * HEAD_DIM:(hh % 2 + 1) * HEAD_DIM, toks(jq)]
    q_sc[HEAD_DIM:LANES, :] = jnp.full((LANES - HEAD_DIM, cols), -1.0, BF16)

    def tok(shape, lane0=0):
        lane = lane0 + lax.broadcasted_iota(jnp.int32, shape, 1)
        return i * tq + (lane // wsub) * SUB_Q + (lane & (SUB_Q - 1))

    nwin = WINDOW // SUB_Q
    kk = lax.broadcasted_iota(jnp.int32, (SUB_Q, wsub), 0)
    r = lax.broadcasted_iota(jnp.int32, (SUB_Q, wsub), 1) & (SUB_Q - 1)

    def chunk(jq, c):
        return i * nsub + jq - nwin + c

    def window_scores(jq):
        q_sub = q_sc[:, jq * wsub:(jq + 1) * wsub]
        for c in range(nwin + 1):
            ci = chunk(jq, c)
            start = pl.multiple_of(jnp.where(ci >= 0, ci * SUB_Q, seq), SUB_Q)
            sw = _dot(k_ref[0, 0, pl.ds(start, SUB_Q), :], q_sub)
            if c == nwin:
                sw = jnp.where(kk <= r, sw, NEG)
            elif c == 0:
                sw = jnp.where(kk > r, sw, NEG)
            w_sc[jq, c * SUB_Q:(c + 1) * SUB_Q, :] = sw

    def window_finish(jq):
        sw = w_sc[jq]
        pw = jnp.exp2(sw - jnp.max(sw, axis=0, keepdims=True)).astype(BF16)
        ow = jnp.zeros((LANES, wsub), F32)
        for c in range(nwin + 1):
            start = pl.multiple_of(jnp.maximum(chunk(jq, c), 0) * SUB_Q, SUB_Q)
            ow = ow + _dot(vwt_ref[0, 0, :, pl.ds(start, SUB_Q)], pw[c * SUB_Q:(c + 1) * SUB_Q])
        o_sc[1, :, jq * wsub:(jq + 1) * wsub] = ow

    s = _dot(kc_ref[0], q_sc[...])
    for jq in range(nsub):
        window_scores(jq)
    last_block = lax.shift_right_arithmetic(tok((1, cols)) - (CMP_LEN - 1), CMP_STRIDE.bit_length() - 1)
    s = jnp.where(lax.broadcasted_iota(jnp.int32, (nc, cols), 0) <= last_block, s, NEG)
    m = jnp.max(s, axis=0, keepdims=True)
    p = jnp.exp2(s - jnp.maximum(m, 0.5 * NEG))
    l = jnp.sum(p, axis=0, keepdims=True)
    inv_l = 1.0 / jnp.where(l > 0.0, l, 1.0)
    o_sc[0] = _dot(vct_ref[0], p.astype(BF16)) * inv_l

    sums = []
    for jq in range(nsub):
        ps = p[:, lanes(jq, 0)] * inv_l[:, lanes(jq, 0)]
        for hh in range(1, HEADS_PER_GROUP):
            ps = ps + p[:, lanes(jq, hh)] * inv_l[:, lanes(jq, hh)]
        sums.append(ps)
    hi, lo = _split_bf16(jnp.concatenate(sums, axis=1))
    ov = ov_ref[...]
    imp = _dot(ov, hi) + _dot(ov, lo)

    for jq in range(nsub - 1):
        window_finish(jq)

    j = lax.broadcasted_iota(jnp.int32, (SEL_SLOTS, tq), 0)
    cur = (i * tq + lax.broadcasted_iota(jnp.int32, (SEL_SLOTS, tq), 1)) // SEL_LEN
    forced = (j == 0) | (j == cur) | (j == cur - 1)
    imp = jnp.where(j <= cur, jnp.where(forced, FORCED_SCORE, imp), -1.0)
    def ranked(n_rows):
        def run():
            sub = SUBLANES
            slabs = [imp[a:a + sub] for a in range(0, n_rows, sub)]
            ranks = [jnp.zeros((sub, tq), F32) for _ in slabs]
            sub_row = lax.broadcasted_iota(jnp.int32, (sub, tq), 0)
            for jj in range(n_rows):
                rowj = jnp.broadcast_to(imp[jj:jj + 1, :], (sub, tq))
                for a, xs in enumerate(slabs):
                    if a * sub > jj:
                        inc = jnp.where(rowj >= xs, 1.0, 0.0)
                    elif a * sub + sub - 1 <= jj:
                        inc = jnp.where(rowj > xs, 1.0, 0.0)
                    else:
                        tie = jnp.where(sub_row > jj - a * sub, 1.0, 0.0)
                        inc = jnp.where(rowj > xs, 1.0, jnp.where(rowj == xs, tie, 0.0))
                    ranks[a] = ranks[a] + inc
            rank = jnp.concatenate(ranks, axis=0)
            kept = jnp.where((rank < float(topn)) & (imp[0:n_rows] >= 0.0), 0.0, -1.0)
            if n_rows == SEL_SLOTS:
                return kept
            return jnp.concatenate([kept, jnp.full((SEL_SLOTS - n_rows, tq), -1.0, F32)], axis=0)
        return run

    row_options = tuple(r for r in (SEL_SLOTS // 2, 3 * SEL_SLOTS // 4) if r > topn) + (SEL_SLOTS,)
    visible = (i + 1) * tq // SEL_LEN
    branch = sum((visible > r).astype(jnp.int32) for r in (topn,) + row_options[:-1])
    flag = lax.switch(branch, [lambda: jnp.where(imp >= 0.0, 0.0, -1.0)]
                      + [ranked(r) for r in row_options]).astype(BF16)

    for jq in range(nsub):
        for hh in range(HEADS_PER_GROUP):
            q_sc[HEAD_DIM:LANES, lanes(jq, hh)] = flag[:, toks(jq)]
    acc_sc[...] = jnp.zeros(acc_sc.shape, F32)

    blocks = [slice(a, a + MXU_COLS) for a in range(0, cols, MXU_COLS)]

    def key_tile(kt):
        return k_ref[0, 0, pl.ds(pl.multiple_of(seq + kt * tk, tk), tk), :]

    def value_tile(kt):
        return vst_ref[0, 0, :, pl.ds(pl.multiple_of(jnp.maximum(kt, 0) * tk, tk), tk)]

    def score_block(k, slot, cs):
        ss = _dot(k, q_sc[:, cs])
        s_sc[slot, :, cs] = ss
        return jnp.max(ss, axis=0, keepdims=True)

    def softmax_block(slot, cs, m_prev, m_blk, diagonal):
        ss = s_sc[slot, :, cs]
        if diagonal:
            m_blk = jnp.max(ss, axis=0, keepdims=True)
        m_new = jnp.maximum(m_prev, m_blk)
        p_sc[slot, :, cs] = jnp.exp2(ss - m_new).astype(BF16)
        return m_new, jnp.exp2(m_prev - m_new)

    def value_block(vt, slot, cs, a_blk):
        acc_sc[:, cs] = acc_sc[:, cs] * a_blk + _dot(vt, p_sc[slot, :, cs])

    def row(parts):
        return jnp.concatenate(parts, axis=1)

    n_tiles = (i * tq) // tk + 1
    k_first = key_tile(0)
    m_first = row([score_block(k_first, 0, cs) for cs in blocks])
    window_finish(nsub - 1)

    @pl.when(n_tiles == 1)
    def _():
        p_sc[1] = jnp.zeros(p_sc.shape[1:], BF16)

    def arm(k, slot, carry, with_value):
        m_run, alpha, m_tile = carry
        kx = key_tile(k)
        vt = value_tile(k - 2) if with_value else None
        m_new, a_new, m_next = [], [], []
        for cs in blocks:
            mn, an = softmax_block(1 - slot, cs, m_run[:, cs], m_tile[:, cs], False)
            m_next.append(score_block(kx, slot, cs))
            if with_value:
                value_block(vt, slot, cs, alpha[:, cs])
            m_new.append(mn)
            a_new.append(an)
        return row(m_new), row(a_new), row(m_next)

    def steady(k, carry):
        return lax.cond(k % 2 == 0, lambda: arm(k, 0, carry, True), lambda: arm(k, 1, carry, True))

    start = (jnp.full((1, cols), NEG, F32), jnp.ones((1, cols), F32), m_first)
    first = lax.cond(n_tiles > 1, lambda: arm(1, 1, start, False), lambda: start)
    m_run, alpha, m_tile = lax.fori_loop(2, n_tiles, steady, first)

    def drain(slot):
        row0 = i * tq - (n_tiles - 1) * tk
        vt_prev, vt_last = value_tile(n_tiles - 2), value_tile(n_tiles - 1)
        for cs in blocks:
            value_block(vt_prev, 1 - slot, cs, alpha[:, cs])
        for jq in range(nsub):
            rows = pl.ds(pl.multiple_of(row0 + jq * SUB_Q, SUB_Q), SUB_Q)
            own = slice(jq * wsub, (jq + 1) * wsub)
            s_sc[slot, rows, own] = jnp.where(kk <= r, s_sc[slot, rows, own], NEG)
        for cs in blocks:
            _, a_last = softmax_block(slot, cs, m_run[:, cs], m_tile[:, cs], True)
            value_block(vt_last, slot, cs, a_last)

    @pl.when((n_tiles - 1) % 2 == 0)
    def _():
        drain(0)

    @pl.when((n_tiles - 1) % 2 == 1)
    def _():
        drain(1)

    gt = gt_ref[0]
    tiles = []
    for jq in range(nsub):
        heads = []
        for hh in range(HEADS_PER_GROUP):
            cs = lanes(jq, hh)
            g_c = gt[hh:hh + 1, toks(jq)]
            g_s = gt[HEADS_PER_GROUP + hh:HEADS_PER_GROUP + hh + 1, toks(jq)]
            g_w = gt[2 * HEADS_PER_GROUP + hh:2 * HEADS_PER_GROUP + hh + 1, toks(jq)]
            heads.append(o_sc[0, 0:HEAD_DIM, cs] * g_c
                         + acc_sc[0:HEAD_DIM, cs] * (g_s / acc_sc[HEAD_DIM:HEAD_DIM + 1, cs])
                         + o_sc[1, 0:HEAD_DIM, cs] * (g_w / o_sc[1, HEAD_DIM:HEAD_DIM + 1, cs]))
        tiles.append(jnp.concatenate(heads, axis=0).T)
    y_ref[0] = jnp.concatenate(tiles, axis=0).astype(BF16)


def _nsa(qt, kcmp, vcmpt, ov, keys, vst, vwt, gt, tq, tk, topn):
    B, _, _, S = qt.shape
    nc = kcmp.shape[1]
    cols = HEADS_PER_GROUP * tq
    kern = functools.partial(_nsa_kernel, tq=tq, tk=tk, topn=topn)
    seq_rows = pl.BlockSpec((1, 1, 2 * S, LANES), lambda b, g, i: (b, g, 0, 0))
    seq_cols = pl.BlockSpec((1, 1, LANES, S), lambda b, g, i: (b, g, 0, 0))
    return pl.pallas_call(
        kern,
        grid=(B, KV_GROUPS, S // tq),
        in_specs=[pl.BlockSpec((1, HEADS_PER_GROUP // 2, LANES, tq), lambda b, g, i: (b, g, 0, i)),
                  pl.BlockSpec((1, nc, LANES), lambda b, g, i: (b * KV_GROUPS + g, 0, 0)),
                  pl.BlockSpec((1, LANES, nc), lambda b, g, i: (b * KV_GROUPS + g, 0, 0)),
                  pl.BlockSpec((SEL_SLOTS, nc), lambda b, g, i: (0, 0)),
                  seq_rows, seq_cols, seq_cols,
                  pl.BlockSpec((1, GATE_ROWS, tq), lambda b, g, i: (b, g, i))],
        out_specs=pl.BlockSpec((1, tq, HEADS_PER_GROUP * HEAD_DIM), lambda b, g, i: (b, i, g)),
        out_shape=jax.ShapeDtypeStruct((B, S, N_HEADS * HEAD_DIM), BF16),
        scratch_shapes=[pltpu.VMEM((LANES, cols), BF16), pltpu.VMEM((LANES, cols), F32),
                        pltpu.VMEM((2, LANES, cols), F32),
                        pltpu.VMEM((tq // SUB_Q, WINDOW + SUB_Q, HEADS_PER_GROUP * SUB_Q), F32),
                        pltpu.VMEM((2, tk, cols), F32), pltpu.VMEM((2, tk, cols), BF16)],
        compiler_params=pltpu.CompilerParams(vmem_limit_bytes=VMEM_LIMIT),
    )(qt, kcmp, vcmpt, ov, keys, vst, vwt, gt)


def _out_ffn_kernel(x_ref, mod_ref, n2_ref, yp_ref, yn_ref, wo_ref, w1_ref, w2_ref, o_ref, *, ff_chunk):
    x = x_ref[0]
    gate1 = mod_ref[0, 0, 2:3, :]
    shift2 = mod_ref[0, 0, 3:4, :]
    scale2 = mod_ref[0, 0, 4:5, :]
    gate2 = mod_ref[0, 0, 5:6, :]
    half = wo_ref.shape[0] // 2
    mix = _dot(yp_ref[0], wo_ref[0:half, :]) + _dot(yn_ref[0], wo_ref[half:2 * half, :])
    x1 = x + gate1 * mix
    ms = jnp.mean(x1 * x1, axis=-1, keepdims=True)
    h2 = ((x1 * lax.rsqrt(ms + EPS) * n2_ref[...]) * (1.0 + scale2) + shift2).astype(BF16)
    d_ff = w1_ref.shape[1]
    ff = jnp.zeros(x.shape, F32)
    for c in range(d_ff // ff_chunk):
        a = jnp.maximum(_dot(h2, w1_ref[:, c * ff_chunk:(c + 1) * ff_chunk]), 0.0)
        ff = ff + _dot((a * a).astype(BF16), w2_ref[c * ff_chunk:(c + 1) * ff_chunk, :])
    o_ref[0] = x1 + gate2 * ff


def _out_ffn(x, mod, layer, n2, yp, yn, wo, w1, w2, tm):
    B, S, D = x.shape
    kern = functools.partial(_out_ffn_kernel, ff_chunk=1024)
    tok = lambda width: pl.BlockSpec((1, tm, width), lambda b, i: (b, i, 0))
    per_layer = lambda a: pl.BlockSpec((None,) + a.shape[1:], lambda b, i: (layer,) + (0,) * (a.ndim - 1),
                                       pipeline_mode=pl.Buffered(1))
    return pl.pallas_call(
        kern,
        grid=(B, S // tm),
        in_specs=[tok(D),
                  pl.BlockSpec((1, 1, 6, D), lambda b, i: (layer, b, 0, 0)),
                  per_layer(n2),
                  tok(512), tok(512),
                  per_layer(wo), per_layer(w1), per_layer(w2)],
        out_specs=tok(D),
        out_shape=jax.ShapeDtypeStruct((B, S, D), F32),
        compiler_params=pltpu.CompilerParams(vmem_limit_bytes=VMEM_LIMIT),
    )(x, mod, n2, yp, yn, wo, w1, w2)


def _rot_cols(w):
    shp = w.shape
    w4 = w.reshape(shp[:-1] + (shp[-1] // HEAD_DIM, 2, HEAD_DIM // 2))
    return jnp.stack([-w4[..., 1, :], w4[..., 0, :]], axis=-2).reshape(shp)


def _swap_halves(g):
    half = HEAD_DIM // 2
    return jnp.concatenate([g[..., half:], g[..., :half]], axis=-1)


def _rope_tables(pos, reps, signed):
    half = HEAD_DIM // 2
    inv = ROPE_THETA ** (-jnp.arange(half, dtype=F32) * 2.0 / HEAD_DIM)
    ang = pos.astype(F32)[:, None] * inv[None, :]
    cos = jnp.tile(jnp.cos(ang), (1, 2 * reps))
    sin = jnp.sin(ang)
    sin = jnp.tile(jnp.concatenate([-sin if signed else sin, sin], axis=1), (1, reps))
    return cos, sin


def _overlap(n_chunks, n_sel):
    cs0 = np.arange(n_chunks) * CMP_STRIDE
    ss0 = np.arange(n_sel) * SEL_LEN
    ov = np.minimum(cs0[:, None] + CMP_LEN, ss0[None, :] + SEL_LEN) - np.maximum(cs0[:, None], ss0[None, :])
    ov = np.clip(ov, 0, None).astype(np.float32) / CMP_LEN
    out = np.zeros((SEL_SLOTS, n_chunks), np.float32)
    out[:n_sel] = ov.T
    return out


def _gate_cols(w_g):
    lead = w_g.shape[:-1]
    by_head = w_g.reshape(lead + (KV_GROUPS, HEADS_PER_GROUP, N_BRANCH))
    by_branch = jnp.swapaxes(by_head, -1, -2).reshape(lead + (KV_GROUPS, N_BRANCH * HEADS_PER_GROUP))
    pad_rows = GATE_ROWS - N_BRANCH * HEADS_PER_GROUP
    grouped = jnp.pad(by_branch, [(0, 0)] * len(lead) + [(0, 0), (0, pad_rows)])
    flat = grouped.reshape(lead + (KV_GROUPS * GATE_ROWS,))
    return jnp.pad(flat, [(0, 0)] * len(lead) + [(0, LANES - KV_GROUPS * GATE_ROWS)])


def kernel(x, c, w_mod, b_mod, norm1, norm2, w_in, pool_w, pool_scale, q_norm, k_norm, cmp_pe, cmp_w1, cmp_w2, w_out, w_ff1, w_ff2):
    B, S, D = x.shape
    depth = w_mod.shape[0]
    n_chunks = S // CMP_STRIDE
    n_sel = S // SEL_LEN
    assert D == 1024 and S % 2048 == 0 and n_sel <= SEL_SLOTS
    topn = min(SEL_TOPN, n_sel)
    tm_in, tm_out, tq, tk = 1024, 1024, 512, 512

    mod = _modulation(c, w_mod, b_mod)

    w_b = w_in[:, :, :C_END].astype(BF16)
    w_g = _gate_cols(w_in[:, :, C_END:]).astype(BF16)
    gq = jnp.tile(q_norm, (1, N_HEADS))[:, None, :]
    gk = jnp.stack([jnp.tile(k_norm[:, 1], (1, 2)), jnp.tile(k_norm[:, 2], (1, 2))], axis=1)
    pad64 = ((0, 0), (0, HEAD_DIM))
    kn = jnp.stack([jnp.pad(k_norm[:, 0], pad64), jnp.pad(_swap_halves(k_norm[:, 0]), pad64)], axis=1)
    w1_k = jnp.pad(cmp_w1[:, 0].astype(BF16), ((0, 0), (0, 0), (0, HEAD_DIM)))
    w1_v = jnp.pad(cmp_w1[:, 1].astype(BF16), ((0, 0), (0, 0), (HEAD_DIM, 0)))
    w1c = jnp.stack([w1_k.reshape(depth, CMP_LEN, HEAD_DIM, LANES),
                     w1_v.reshape(depth, CMP_LEN, HEAD_DIM, LANES)],
                    axis=2).reshape(depth, CMP_LEN, LANES, LANES)
    zero_w2 = jnp.zeros_like(cmp_w2[:, 0])
    w2c = jnp.concatenate([
        jnp.concatenate([cmp_w2[:, 0], zero_w2, _rot_cols(cmp_w2[:, 0]), zero_w2, zero_w2, zero_w2], axis=-1),
        jnp.concatenate([zero_w2, zero_w2, zero_w2, zero_w2, cmp_w2[:, 1], zero_w2], axis=-1)],
        axis=-2).astype(BF16)
    pe = jnp.concatenate([cmp_pe[:, 0], cmp_pe[:, 1]], axis=-1)
    pool_wb = pool_w.astype(BF16)
    wo_b, w1_b, w2_b = w_out.astype(BF16), w_ff1.astype(BF16), w_ff2.astype(BF16)

    cos, sin = _rope_tables(jnp.arange(S), 2, signed=True)
    cos_c, sin_c = _rope_tables(jnp.arange(n_chunks) * CMP_STRIDE + (CMP_LEN - 1), 2, signed=False)
    bd = jnp.asarray(np.kron(np.eye(N_HEADS, dtype=np.float32),
                             np.full((HEAD_DIM, HEAD_DIM), 1.0 / HEAD_DIM, np.float32)), BF16)
    ov = jnp.asarray(_overlap(n_chunks, n_sel), BF16)

    for l in range(depth):
        y_pool, qt, keys, vst, vwt, kvc, gt = _inproj(
            x, mod, l, norm1[:, None, :], w_b, w_g, cos, sin, gq, gk, bd,
            pool_wb, pool_scale[:, None, :], tm_in)
        kcmp, vcmpt = _compress(kvc, l, pe, w1c, w2c, kn, cos_c, sin_c)
        y_nsa = _nsa(qt, kcmp, vcmpt, ov, keys.reshape(B, KV_GROUPS, 2 * S, LANES), vst, vwt, gt,
                     tq, tk, topn)
        x = _out_ffn(x, mod, l, norm2[:, None, :], y_pool, y_nsa, wo_b, w1_b, w2_b, tm_out)
    return x
```
